```python
import jax, jax.numpy as jnp
from jax import lax
import numpy as np

D_MODEL = 1024
BATCH = 16
SEQ = 2048
DEPTH = 4

CONV_WIDTH = D_MODEL
CONV_K = 3
SGU_WIDTH = D_MODEL
SGU_CHUNK = 128
SGU_HEADS = 8
SGU_HEAD_DIM = SGU_WIDTH // SGU_HEADS
LRU_WIDTH = D_MODEL
LRU_HEADS = 16
LRU_HEAD_DIM = LRU_WIDTH // LRU_HEADS
LRU_CONV_K = 4
LRU_C = 8.0
N_BRANCH = 3
IN_COLS = 4 * CONV_WIDTH + 3 * SGU_WIDTH + 2 * LRU_WIDTH + N_BRANCH * D_MODEL
EPS = 1e-6

kernel_name = "hybrid_gated_conv_sgu_rglru"


def rms_norm(x, g):
    xf = x.astype(jnp.float32)
    y = xf * lax.rsqrt(jnp.mean(xf * xf, axis=-1, keepdims=True) + EPS)
    return (y * g.astype(jnp.float32)).astype(x.dtype)


def causal_dwconv(u, w):
    k = w.shape[0]
    s = u.shape[1]
    up = jnp.pad(u, ((0, 0), (k - 1, 0), (0, 0)))
    return sum(w[j] * up[:, j:j + s] for j in range(k))


def short_conv_mixer(x_a, b_gate, c_gate, w_conv):
    return b_gate * causal_dwconv(c_gate * x_a, w_conv)


def sgu_mixer(u, v, w_s, b_s):
    bsz, s, _ = v.shape
    nc = s // SGU_CHUNK
    vh = v.reshape(bsz, nc, SGU_CHUNK, SGU_HEADS, SGU_HEAD_DIM)
    vf = vh.astype(jnp.float32)
    mu = jnp.mean(vf, axis=-1, keepdims=True)
    var = jnp.mean(jnp.square(vf - mu), axis=-1, keepdims=True)
    vn = ((vf - mu) * lax.rsqrt(var + EPS)).astype(v.dtype)
    causal = jnp.tril(jnp.ones((SGU_CHUNK, SGU_CHUNK), dtype=bool))
    ws = jnp.where(causal[None], w_s, jnp.zeros_like(w_s))
    z = jnp.einsum("hts,bnshd->bnthd", ws, vn) + b_s.T[None, None, :, :, None]
    return u * z.reshape(bsz, s, SGU_WIDTH)


def rg_lru_mixer(x_r, w_conv, b_conv, w_a, b_a, w_x, b_x, lam):
    bsz, s, _ = x_r.shape
    xc = causal_dwconv(x_r, w_conv) + b_conv
    xh = xc.reshape(bsz, s, LRU_HEADS, LRU_HEAD_DIM)
    r = jax.nn.sigmoid(jnp.einsum("bshd,hde->bshe", xh, w_a) + b_a).reshape(bsz, s, LRU_WIDTH)
    i = jax.nn.sigmoid(jnp.einsum("bshd,hde->bshe", xh, w_x) + b_x).reshape(bsz, s, LRU_WIDTH)
    log_a = -LRU_C * r.astype(jnp.float32) * jax.nn.softplus(-lam.astype(jnp.float32))
    a = jnp.exp(log_a)
    mult = jnp.sqrt(-jnp.expm1(2.0 * log_a))
    b = mult * (i * xc).astype(jnp.float32)

    def combine(left, right):
        a_l, b_l = left
        a_r, b_r = right
        return a_l * a_r, a_r * b_l + b_r

    _, h = lax.associative_scan(combine, (a, b), axis=1)
    return h.astype(x_r.dtype)


def hybrid_layer(x, c_act, gain, w_mod, b_mod, w_in, w_out, conv_a_w, sgu_w, sgu_b,
                 lru_conv_w, lru_conv_b, lru_wa, lru_ba, lru_wx, lru_bx, lru_lambda):
    mod = c_act @ w_mod + b_mod
    shift, scale, gate = jnp.split(mod, 3, axis=-1)
    h = rms_norm(x, gain) * (1.0 + scale[:, None, :]) + shift[:, None, :]
    proj = h @ w_in
    sizes = (CONV_WIDTH,) * 4 + (SGU_WIDTH,) * 3 + (LRU_WIDTH,) * 2 + (D_MODEL,) * N_BRANCH
    splits = np.cumsum(sizes)[:-1].tolist()
    (a_x, a_b, a_c, a_z, s_u, s_v, s_z, r_x, r_z, g_a, g_s, g_r) = jnp.split(proj, splits, axis=-1)
    y_a = jax.nn.silu(a_z) * short_conv_mixer(a_x, a_b, a_c, conv_a_w)
    y_s = jax.nn.silu(s_z) * sgu_mixer(s_u, s_v, sgu_w, sgu_b)
    y_r = jax.nn.silu(r_z) * rg_lru_mixer(r_x, lru_conv_w, lru_conv_b, lru_wa, lru_ba,
                                          lru_wx, lru_bx, lru_lambda)
    merged = jax.nn.sigmoid(g_a) * y_a + jax.nn.sigmoid(g_s) * y_s + jax.nn.sigmoid(g_r) * y_r
    return x + gate[:, None, :] * (merged @ w_out)


def setup_inputs(seed: int = 0) -> dict:
    key = jax.random.key(seed)
    ks = jax.random.split(key, 20)
    nrm = jax.random.normal
    d = D_MODEL
    x = nrm(ks[0], (BATCH, SEQ, d), jnp.float32)
    c = nrm(ks[1], (BATCH, d), jnp.float32)
    norm_gain = 1.0 + 0.05 * nrm(ks[2], (DEPTH, d), jnp.float32)
    w_mod = 0.1 * d ** -0.5 * nrm(ks[3], (DEPTH, d, 3 * d), jnp.float32)
    b_mod = 0.02 * nrm(ks[4], (DEPTH, 3 * d), jnp.float32)
    w_in = d ** -0.5 * nrm(ks[5], (DEPTH, d, IN_COLS), jnp.float32)
    w_out = d ** -0.5 * nrm(ks[6], (DEPTH, d, d), jnp.float32)
    conv_a_w = CONV_K ** -0.5 * nrm(ks[7], (DEPTH, CONV_K, CONV_WIDTH), jnp.float32)
    sgu_w = SGU_CHUNK ** -0.5 * nrm(ks[8], (DEPTH, SGU_HEADS, SGU_CHUNK, SGU_CHUNK), jnp.float32)
    sgu_b = 1.0 + 0.1 * nrm(ks[9], (DEPTH, SGU_HEADS, SGU_CHUNK), jnp.float32)
    lru_conv_w = LRU_CONV_K ** -0.5 * nrm(ks[10], (DEPTH, LRU_CONV_K, LRU_WIDTH), jnp.float32)
    lru_conv_b = 0.02 * nrm(ks[11], (DEPTH, LRU_WIDTH), jnp.float32)
    lru_wa = LRU_HEAD_DIM ** -0.5 * nrm(ks[12], (DEPTH, LRU_HEADS, LRU_HEAD_DIM, LRU_HEAD_DIM), jnp.float32)
    lru_ba = 0.02 * nrm(ks[13], (DEPTH, LRU_HEADS, LRU_HEAD_DIM), jnp.float32)
    lru_wx = LRU_HEAD_DIM ** -0.5 * nrm(ks[14], (DEPTH, LRU_HEADS, LRU_HEAD_DIM, LRU_HEAD_DIM), jnp.float32)
    lru_bx = 0.02 * nrm(ks[15], (DEPTH, LRU_HEADS, LRU_HEAD_DIM), jnp.float32)
    a_c = jax.random.uniform(ks[16], (DEPTH, LRU_WIDTH), jnp.float32, 0.9, 0.999)
    a_base = a_c ** (1.0 / LRU_C)
    lru_lambda = jnp.log(a_base) - jnp.log1p(-a_base)
    final_gain = 1.0 + 0.05 * nrm(ks[17], (d,), jnp.float32)
    return {"x": x, "c": c, "norm_gain": norm_gain, "w_mod": w_mod, "b_mod": b_mod,
            "w_in": w_in, "w_out": w_out, "conv_a_w": conv_a_w, "sgu_w": sgu_w, "sgu_b": sgu_b,
            "lru_conv_w": lru_conv_w, "lru_conv_b": lru_conv_b, "lru_wa": lru_wa, "lru_ba": lru_ba,
            "lru_wx": lru_wx, "lru_bx": lru_bx, "lru_lambda": lru_lambda, "final_gain": final_gain}


def reference(x, c, norm_gain, w_mod, b_mod, w_in, w_out, conv_a_w, sgu_w, sgu_b,
              lru_conv_w, lru_conv_b, lru_wa, lru_ba, lru_wx, lru_bx, lru_lambda, final_gain):
    c_act = jax.nn.silu(c)
    for l in range(DEPTH):
        x = hybrid_layer(x, c_act, norm_gain[l], w_mod[l], b_mod[l], w_in[l], w_out[l],
                         conv_a_w[l], sgu_w[l], sgu_b[l], lru_conv_w[l], lru_conv_b[l],
                         lru_wa[l], lru_ba[l], lru_wx[l], lru_bx[l], lru_lambda[l])
    return rms_norm(x, final_gain)
```

```python
import functools

import jax
import jax.numpy as jnp
from jax import lax
from jax.experimental import pallas as pl
from jax.experimental.pallas import tpu as pltpu

D_MODEL = 1024
SGU_CHUNK = 128
SGU_HEADS = 8
LRU_HEADS = 16
LRU_HEAD_DIM = 64
CONV_K = 3
LRU_CONV_K = 4
LRU_C = 8.0
EPS = 1e-6

SUBLANES = 8
LANES = 128
N_SLICES = 12
(A_X, A_B, A_C, A_Z, S_U, S_V, S_Z, R_X, R_Z, G_A, G_S, G_R) = range(N_SLICES)
N_COL_BLOCKS = D_MODEL // LANES
BLOCK_COLS = N_SLICES * LANES
SEQ_TILE = 256
VMEM_LIMIT_BYTES = 56 * 1024 * 1024


def _sigmoid(v):
    return 1.0 / (1.0 + jnp.exp(-v))


def _silu(v):
    return v * _sigmoid(v)


def _mod_kernel(c_ref, w_ref, b_ref, o_ref):
    c = c_ref[...]
    o_ref[0, 0] = jnp.dot(_silu(c), w_ref[0], preferred_element_type=jnp.float32) + b_ref[0, 0]


def _modulation(c, w_mod, b_mod):
    depth, d, _ = w_mod.shape
    bsz = c.shape[0]
    return pl.pallas_call(
        _mod_kernel,
        grid=(depth, 3),
        in_specs=[
            pl.BlockSpec((bsz, d), lambda l, k: (0, 0)),
            pl.BlockSpec((1, d, d), lambda l, k: (l, 0, k)),
            pl.BlockSpec((1, 1, 1, d), lambda l, k: (l, k, 0, 0)),
        ],
        out_specs=pl.BlockSpec((1, 1, bsz, d), lambda l, k: (l, k, 0, 0)),
        out_shape=jax.ShapeDtypeStruct((depth, 3, bsz, d), jnp.float32),
        name="adaln_modulation",
    )(c, w_mod, b_mod.reshape(depth, 3, 1, d))


def _shift_rows(v3, carry, k, row_in_group):
    rolled = pltpu.roll(v3, k, axis=1)
    rolled_carry = pltpu.roll(carry, k, axis=0)
    prev = jnp.concatenate([rolled_carry[None], rolled[:-1]], axis=0)
    return jnp.where(row_in_group >= k, rolled, prev)


def _causal_conv(v3, carry, w_ref, col, row_in_group):
    k_taps = w_ref.shape[0]
    acc = w_ref[k_taps - 1:k_taps, col:col + LANES] * v3
    for j in range(k_taps - 1):
        delay = k_taps - 1 - j
        acc = acc + w_ref[j:j + 1, col:col + LANES] * _shift_rows(v3, carry, delay, row_in_group)
    return acc


def _linear_scan(a3, b3, h0, row_in_group):
    for k in (1, 2, 4):
        keep = row_in_group >= k
        a_prev = jnp.where(keep, pltpu.roll(a3, k, axis=1), 1.0)
        b_prev = jnp.where(keep, pltpu.roll(b3, k, axis=1), 0.0)
        b3 = a3 * b_prev + b3
        a3 = a3 * a_prev
    groups = []
    h = h0
    for g in range(a3.shape[0]):
        hg = a3[g] * h + b3[g]
        groups.append(hg)
        h = hg[SUBLANES - 1:SUBLANES, :]
    return jnp.stack(groups, axis=0), h


def _layer_kernel(x_ref, mod_ref, gain_ref, w_in_ref, w_out_ref, conv_a_ref, sgu_w_ref,
                  sgu_b_ref, lru_cw_ref, lru_cb_ref, lru_w_ref, lru_ba_ref, lru_bx_ref,
                  lam_ref, fgain_ref, o_ref,
                  h_scr, m_scr, p_scr, cx_carry, rx_carry, h_carry, *, final_norm):
    tile = x_ref.shape[1]
    groups = tile // SUBLANES
    n_chunks = tile // SGU_CHUNK
    b = pl.program_id(0)

    @pl.when(pl.program_id(1) == 0)
    def _():
        cx_carry[...] = jnp.zeros_like(cx_carry)
        rx_carry[...] = jnp.zeros_like(rx_carry)
        h_carry[...] = jnp.zeros_like(h_carry)

    shift = mod_ref[0, pl.ds(b, 1), :]
    scale = mod_ref[1, pl.ds(b, 1), :]
    gate = mod_ref[2, pl.ds(b, 1), :]

    x = x_ref[0]
    inv = lax.rsqrt(jnp.mean(x * x, axis=-1, keepdims=True) + EPS)
    h = (x * inv) * (gain_ref[...] * (1.0 + scale)) + shift
    h_scr[...] = h.astype(jnp.bfloat16)

    row_in_group = lax.broadcasted_iota(jnp.int32, (groups, SUBLANES, LANES), 1)
    causal = (lax.broadcasted_iota(jnp.int32, (SGU_CHUNK, SGU_CHUNK), 0)
              >= lax.broadcasted_iota(jnp.int32, (SGU_CHUNK, SGU_CHUNK), 1))

    def project(j):
        p_scr[j % 2] = jnp.dot(h_scr[...], w_in_ref[j], preferred_element_type=jnp.float32)

    def mix(j):
        col = j * LANES
        p = p_scr.at[j % 2]

        def sl(k):
            return p[:, k * LANES:(k + 1) * LANES]

        def sl3(k):
            return sl(k).reshape(groups, SUBLANES, LANES)

        cx = sl3(A_C) * sl3(A_X)
        conv = _causal_conv(cx, cx_carry[:, col:col + LANES], conv_a_ref, col, row_in_group)
        cx_carry[:, col:col + LANES] = cx[groups - 1]
        y_a = (_silu(sl(A_Z)) * sl(A_B)) * conv.reshape(tile, LANES)
        merged = _sigmoid(sl(G_A)) * y_a

        v = sl(S_V)
        mu = jnp.mean(v, axis=-1, keepdims=True)
        vc = v - mu
        var = jnp.mean(vc * vc, axis=-1, keepdims=True)
        vn = (vc * lax.rsqrt(var + EPS)).astype(jnp.bfloat16)
        vn_wide = jnp.concatenate(
            [vn[n * SGU_CHUNK:(n + 1) * SGU_CHUNK] for n in range(n_chunks)], axis=1)
        w_s = jnp.where(causal, sgu_w_ref[j], 0.0).astype(jnp.bfloat16)
        z_wide = jnp.dot(w_s, vn_wide, preferred_element_type=jnp.float32)
        z = jnp.concatenate(
            [z_wide[:, n * LANES:(n + 1) * LANES] + sgu_b_ref[j] for n in range(n_chunks)], axis=0)
        y_s = (_silu(sl(S_Z)) * sl(S_U)) * z
        merged = merged + _sigmoid(sl(G_S)) * y_s

        rx = sl3(R_X)
        xc3 = (_causal_conv(rx, rx_carry[:, col:col + LANES], lru_cw_ref, col, row_in_group)
               + lru_cb_ref[:, col:col + LANES])
        rx_carry[:, col:col + LANES] = rx[groups - 1]
        xc = xc3.reshape(tile, LANES)
        pre = jnp.dot(xc.astype(jnp.bfloat16), lru_w_ref[j], preferred_element_type=jnp.float32)
        r = _sigmoid(pre[:, :LANES] + lru_ba_ref[:, col:col + LANES])
        i = _sigmoid(pre[:, LANES:] + lru_bx_ref[:, col:col + LANES])
        lam = lam_ref[:, col:col + LANES]
        neg_c_softplus = -LRU_C * (jnp.maximum(-lam, 0.0) + jnp.log1p(jnp.exp(-jnp.abs(lam))))
        a = jnp.exp(neg_c_softplus * r)
        bb = jnp.sqrt(1.0 - a * a) * (i * xc)
        hs3, h_last = _linear_scan(a.reshape(groups, SUBLANES, LANES),
                                   bb.reshape(groups, SUBLANES, LANES),
                                   h_carry[:, col:col + LANES], row_in_group)
        h_carry[:, col:col + LANES] = h_last
        y_r = _silu(sl(R_Z)) * hs3.reshape(tile, LANES)
        merged = merged + _sigmoid(sl(G_R)) * y_r

        m_scr[:, col:col + LANES] = merged.astype(jnp.bfloat16)

    project(0)
    for j in range(N_COL_BLOCKS):
        if j + 1 < N_COL_BLOCKS:
            project(j + 1)
        mix(j)

    out = x_ref[0] + gate * jnp.dot(m_scr[...], w_out_ref[...], preferred_element_type=jnp.float32)
    if final_norm:
        inv_o = lax.rsqrt(jnp.mean(out * out, axis=-1, keepdims=True) + EPS)
        out = (out * inv_o) * fgain_ref[...]
    o_ref[0] = out


def _resident(shape):
    zeros = (0,) * len(shape)
    return pl.BlockSpec(shape, lambda b, s: zeros, pipeline_mode=pl.Buffered(1))


def _layer(x, mod_l, gain, w_in_b, w_out_b, conv_a_w, sgu_w, sgu_b_wide, lru_conv_w, lru_conv_b,
           lru_w, lru_ba, lru_bx, lam, final_gain, *, final_norm):
    bsz, seq, d = x.shape
    tile = SEQ_TILE
    operands = (mod_l, gain, w_in_b, w_out_b, conv_a_w, sgu_w, sgu_b_wide, lru_conv_w,
                lru_conv_b, lru_w, lru_ba, lru_bx, lam, final_gain)
    return pl.pallas_call(
        functools.partial(_layer_kernel, final_norm=final_norm),
        grid=(bsz, seq // tile),
        in_specs=[pl.BlockSpec((1, tile, d), lambda b, s: (b, s, 0))]
        + [_resident(op.shape) for op in operands],
        out_specs=pl.BlockSpec((1, tile, d), lambda b, s: (b, s, 0)),
        out_shape=jax.ShapeDtypeStruct(x.shape, x.dtype),
        scratch_shapes=[
            pltpu.VMEM((tile, d), jnp.bfloat16),
            pltpu.VMEM((tile, d), jnp.bfloat16),
            pltpu.VMEM((2, tile, BLOCK_COLS), jnp.float32),
            pltpu.VMEM((SUBLANES, d), jnp.float32),
            pltpu.VMEM((SUBLANES, d), jnp.float32),
            pltpu.VMEM((1, d), jnp.float32),
        ],
        compiler_params=pltpu.CompilerParams(
            dimension_semantics=("arbitrary", "arbitrary"),
            vmem_limit_bytes=VMEM_LIMIT_BYTES),
        name="hybrid_mixer_layer",
    )(x, *operands)


def _block_columns(w_in):
    d = w_in.shape[0]
    w = w_in.reshape(d, N_SLICES, N_COL_BLOCKS, LANES)
    return jnp.transpose(w, (2, 0, 1, 3)).reshape(N_COL_BLOCKS, d, BLOCK_COLS)


def _pair_block_diag(w_a, w_x):
    def pairs(w):
        w = w.reshape(LRU_HEADS // 2, 2, LRU_HEAD_DIM, LRU_HEAD_DIM)
        zero = jnp.zeros_like(w[:, 0])
        top = jnp.concatenate([w[:, 0], zero], axis=-1)
        bottom = jnp.concatenate([zero, w[:, 1]], axis=-1)
        return jnp.concatenate([top, bottom], axis=-2)
    return jnp.concatenate([pairs(w_a), pairs(w_x)], axis=-1)


def kernel(x, c, norm_gain, w_mod, b_mod, w_in, w_out, conv_a_w, sgu_w, sgu_b, lru_conv_w,
           lru_conv_b, lru_wa, lru_ba, lru_wx, lru_bx, lru_lambda, final_gain):
    depth = w_in.shape[0]
    d = x.shape[-1]
    mod = _modulation(c, w_mod, b_mod)
    for l in range(depth):
        x = _layer(
            x, mod[l], norm_gain[l].reshape(1, d),
            _block_columns(w_in[l]).astype(jnp.bfloat16), w_out[l].astype(jnp.bfloat16),
            conv_a_w[l], sgu_w[l],
            jnp.broadcast_to(sgu_b[l][:, :, None], (SGU_HEADS, SGU_CHUNK, LANES)),
            lru_conv_w[l], lru_conv_b[l].reshape(1, d),
            _pair_block_diag(lru_wa[l], lru_wx[l]).astype(jnp.bfloat16),
            lru_ba[l].reshape(1, d), lru_bx[l].reshape(1, d), lru_lambda[l].reshape(1, d),
            final_gain.reshape(1, d), final_norm=(l == depth - 1))
    return x
```

```python
import functools

import jax
import jax.numpy as jnp
from jax import lax
from jax.experimental import pallas as pl
from jax.experimental.pallas import tpu as pltpu

D_MODEL = 1024
SGU_CHUNK = 128
SGU_HEADS = 8
LRU_HEADS = 16
LRU_HEAD_DIM = 64
CONV_K = 3
LRU_CONV_K = 4
LRU_C = 8.0
EPS = 1e-6

SUBLANES = 8
LANES = 128
N_SLICES = 12
(A_X, A_B, A_C, A_Z, S_U, S_V, S_Z, R_X, R_Z, G_A, G_S, G_R) = range(N_SLICES)
N_COL_BLOCKS = D_MODEL // LANES
BLOCK_COLS = N_SLICES * LANES
SEQ_TILE = 512
VMEM_LIMIT_BYTES = 56 * 1024 * 1024


def _sigmoid(v):
    return 1.0 / (1.0 + jnp.exp(-v))


def _silu(v):
    return v * _sigmoid(v)


def _mod_kernel(c_ref, w_ref, b_ref, o_ref):
    c = c_ref[...]
    o_ref[0, 0] = jnp.dot(_silu(c), w_ref[0], preferred_element_type=jnp.float32) + b_ref[0, 0]


def _modulation(c, w_mod, b_mod):
    depth, d, _ = w_mod.shape
    bsz = c.shape[0]
    return pl.pallas_call(
        _mod_kernel,
        grid=(depth, 3),
        in_specs=[
            pl.BlockSpec((bsz, d), lambda l, k: (0, 0)),
            pl.BlockSpec((1, d, d), lambda l, k: (l, 0, k)),
            pl.BlockSpec((1, 1, 1, d), lambda l, k: (l, k, 0, 0)),
        ],
        out_specs=pl.BlockSpec((1, 1, bsz, d), lambda l, k: (l, k, 0, 0)),
        out_shape=jax.ShapeDtypeStruct((depth, 3, bsz, d), jnp.float32),
        name="adaln_modulation",
    )(c, w_mod, b_mod.reshape(depth, 3, 1, d))


def _shift_rows(v3, carry, k, row_in_group):
    rolled = pltpu.roll(v3, k, axis=1)
    rolled_carry = pltpu.roll(carry, k, axis=0)
    prev = jnp.concatenate([rolled_carry[None], rolled[:-1]], axis=0)
    return jnp.where(row_in_group >= k, rolled, prev)


def _causal_conv(v3, carry, w_ref, col, row_in_group):
    k_taps = w_ref.shape[0]
    acc = w_ref[k_taps - 1:k_taps, col:col + LANES] * v3
    for j in range(k_taps - 1):
        delay = k_taps - 1 - j
        acc = acc + w_ref[j:j + 1, col:col + LANES] * _shift_rows(v3, carry, delay, row_in_group)
    return acc


def _linear_scan(a3, b3, h0, row_in_group):
    for k in (1, 2, 4):
        keep = row_in_group >= k
        a_prev = jnp.where(keep, pltpu.roll(a3, k, axis=1), 1.0)
        b_prev = jnp.where(keep, pltpu.roll(b3, k, axis=1), 0.0)
        b3 = a3 * b_prev + b3
        a3 = a3 * a_prev
    groups = []
    h = h0
    for g in range(a3.shape[0]):
        hg = a3[g] * h + b3[g]
        groups.append(hg)
        h = hg[SUBLANES - 1:SUBLANES, :]
    return jnp.stack(groups, axis=0), h


def _layer_kernel(x_ref, mod_ref, gain_ref, w_in_ref, w_out_ref, conv_a_ref, sgu_w_ref,
                  sgu_b_ref, lru_cw_ref, lru_cb_ref, lru_w_ref, lru_ba_ref, lru_bx_ref,
                  lam_ref, fgain_ref, o_ref,
                  h_scr, m_scr, p_scr, cx_carry, rx_carry, h_carry, *, final_norm):
    (mod_ref, gain_ref, w_in_ref, w_out_ref, conv_a_ref, sgu_w_ref, sgu_b_ref, lru_cw_ref,
     lru_cb_ref, lru_w_ref, lru_ba_ref, lru_bx_ref, lam_ref) = (
        r.at[0] for r in (mod_ref, gain_ref, w_in_ref, w_out_ref, conv_a_ref, sgu_w_ref,
                          sgu_b_ref, lru_cw_ref, lru_cb_ref, lru_w_ref, lru_ba_ref, lru_bx_ref,
                          lam_ref))
    tile = x_ref.shape[1]
    groups = tile // SUBLANES
    n_chunks = tile // SGU_CHUNK
    b = pl.program_id(0)

    @pl.when(pl.program_id(1) == 0)
    def _():
        cx_carry[...] = jnp.zeros_like(cx_carry)
        rx_carry[...] = jnp.zeros_like(rx_carry)
        h_carry[...] = jnp.zeros_like(h_carry)

    shift = mod_ref[0, pl.ds(b, 1), :]
    scale = mod_ref[1, pl.ds(b, 1), :]
    gate = mod_ref[2, pl.ds(b, 1), :]

    x = x_ref[0]
    inv = lax.rsqrt(jnp.mean(x * x, axis=-1, keepdims=True) + EPS)
    h = (x * inv) * (gain_ref[...] * (1.0 + scale)) + shift
    h_scr[...] = h.astype(jnp.bfloat16)

    row_in_group = lax.broadcasted_iota(jnp.int32, (groups, SUBLANES, LANES), 1)
    causal = (lax.broadcasted_iota(jnp.int32, (SGU_CHUNK, SGU_CHUNK), 0)
              >= lax.broadcasted_iota(jnp.int32, (SGU_CHUNK, SGU_CHUNK), 1))

    def project(j):
        p_scr[j % 2] = jnp.dot(h_scr[...], w_in_ref[j], preferred_element_type=jnp.float32)

    def mix(j):
        col = j * LANES
        p = p_scr.at[j % 2]

        def sl(k):
            return p[:, k * LANES:(k + 1) * LANES]

        def sl3(k):
            return sl(k).reshape(groups, SUBLANES, LANES)

        cx = sl3(A_C) * sl3(A_X)
        conv = _causal_conv(cx, cx_carry[:, col:col + LANES], conv_a_ref, col, row_in_group)
        cx_carry[:, col:col + LANES] = cx[groups - 1]
        y_a = (_silu(sl(A_Z)) * sl(A_B)) * conv.reshape(tile, LANES)
        merged = _sigmoid(sl(G_A)) * y_a

        v = sl(S_V)
        mu = jnp.mean(v, axis=-1, keepdims=True)
        vc = v - mu
        var = jnp.mean(vc * vc, axis=-1, keepdims=True)
        vn = (vc * lax.rsqrt(var + EPS)).astype(jnp.bfloat16)
        vn_wide = jnp.concatenate(
            [vn[n * SGU_CHUNK:(n + 1) * SGU_CHUNK] for n in range(n_chunks)], axis=1)
        w_s = jnp.where(causal, sgu_w_ref[j], 0.0).astype(jnp.bfloat16)
        z_wide = jnp.dot(w_s, vn_wide, preferred_element_type=jnp.float32)
        z = jnp.concatenate(
            [z_wide[:, n * LANES:(n + 1) * LANES] + sgu_b_ref[j] for n in range(n_chunks)], axis=0)
        y_s = (_silu(sl(S_Z)) * sl(S_U)) * z
        merged = merged + _sigmoid(sl(G_S)) * y_s

        rx = sl3(R_X)
        xc3 = (_causal_conv(rx, rx_carry[:, col:col + LANES], lru_cw_ref, col, row_in_group)
               + lru_cb_ref[:, col:col + LANES])
        rx_carry[:, col:col + LANES] = rx[groups - 1]
        xc = xc3.reshape(tile, LANES)
        pre = jnp.dot(xc.astype(jnp.bfloat16), lru_w_ref[j], preferred_element_type=jnp.float32)
        r = _sigmoid(pre[:, :LANES] + lru_ba_ref[:, col:col + LANES])
        i = _sigmoid(pre[:, LANES:] + lru_bx_ref[:, col:col + LANES])
        lam = lam_ref[:, col:col + LANES]
        neg_c_softplus = -LRU_C * (jnp.maximum(-lam, 0.0) + jnp.log1p(jnp.exp(-jnp.abs(lam))))
        a = jnp.exp(neg_c_softplus * r)
        bb = jnp.sqrt(1.0 - a * a) * (i * xc)
        hs3, h_last = _linear_scan(a.reshape(groups, SUBLANES, LANES),
                                   bb.reshape(groups, SUBLANES, LANES),
                                   h_carry[:, col:col + LANES], row_in_group)
        h_carry[:, col:col + LANES] = h_last
        y_r = _silu(sl(R_Z)) * hs3.reshape(tile, LANES)
        merged = merged + _sigmoid(sl(G_R)) * y_r

        m_scr[:, col:col + LANES] = merged.astype(jnp.bfloat16)

    project(0)
    for j in range(N_COL_BLOCKS):
        if j + 1 < N_COL_BLOCKS:
            project(j + 1)
        mix(j)

    out = x_ref[0] + gate * jnp.dot(m_scr[...], w_out_ref[...], preferred_element_type=jnp.float32)
    if final_norm:
        inv_o = lax.rsqrt(jnp.mean(out * out, axis=-1, keepdims=True) + EPS)
        out = (out * inv_o) * fgain_ref[...]
    o_ref[0] = out


def _layer_resident(arr, layer):
    index = (layer,) + (0,) * (arr.ndim - 1)
    return pl.BlockSpec((1,) + arr.shape[1:], lambda b, s: index, pipeline_mode=pl.Buffered(1))


def _layer(x, layer, stacked, final_gain, *, final_norm):
    bsz, seq, d = x.shape
    tile = SEQ_TILE
    return pl.pallas_call(
        functools.partial(_layer_kernel, final_norm=final_norm),
        grid=(bsz, seq // tile),
        in_specs=[pl.BlockSpec((1, tile, d), lambda b, s: (b, s, 0))]
        + [_layer_resident(arr, layer) for arr in stacked]
        + [pl.BlockSpec(final_gain.shape, lambda b, s: (0, 0), pipeline_mode=pl.Buffered(1))],
        out_specs=pl.BlockSpec((1, tile, d), lambda b, s: (b, s, 0)),
        out_shape=jax.ShapeDtypeStruct(x.shape, x.dtype),
        scratch_shapes=[
            pltpu.VMEM((tile, d), jnp.bfloat16),
            pltpu.VMEM((tile, d), jnp.bfloat16),
            pltpu.VMEM((2, tile, BLOCK_COLS), jnp.float32),
            pltpu.VMEM((SUBLANES, d), jnp.float32),
            pltpu.VMEM((SUBLANES, d), jnp.float32),
            pltpu.VMEM((1, d), jnp.float32),
        ],
        compiler_params=pltpu.CompilerParams(
            dimension_semantics=("arbitrary", "arbitrary"),
            vmem_limit_bytes=VMEM_LIMIT_BYTES),
        name="hybrid_mixer_layer",
    )(x, *stacked, final_gain)


def _block_columns_kernel(*refs):
    o_ref = refs[-1]
    for k in range(N_SLICES):
        o_ref[0, 0, :, k * LANES:(k + 1) * LANES] = refs[k][0].astype(jnp.bfloat16)


def _slice_block_index(l, j, *, k):
    return (l, 0, k * N_COL_BLOCKS + j)


def _block_columns(w_in):
    depth, d, _ = w_in.shape
    return pl.pallas_call(
        _block_columns_kernel,
        grid=(depth, N_COL_BLOCKS),
        in_specs=[pl.BlockSpec((1, d, LANES), functools.partial(_slice_block_index, k=k))
                  for k in range(N_SLICES)],
        out_specs=pl.BlockSpec((1, 1, d, BLOCK_COLS), lambda l, j: (l, j, 0, 0)),
        out_shape=jax.ShapeDtypeStruct((depth, N_COL_BLOCKS, d, BLOCK_COLS), jnp.bfloat16),
        name="block_columns_bf16",
    )(*([w_in] * N_SLICES))


def _pair_block_diag(w_a, w_x):
    def pairs(w):
        w = w.reshape(w.shape[0], LRU_HEADS // 2, 2, LRU_HEAD_DIM, LRU_HEAD_DIM)
        zero = jnp.zeros_like(w[:, :, 0])
        top = jnp.concatenate([w[:, :, 0], zero], axis=-1)
        bottom = jnp.concatenate([zero, w[:, :, 1]], axis=-1)
        return jnp.concatenate([top, bottom], axis=-2)
    return jnp.concatenate([pairs(w_a), pairs(w_x)], axis=-1)


def kernel(x, c, norm_gain, w_mod, b_mod, w_in, w_out, conv_a_w, sgu_w, sgu_b, lru_conv_w,
           lru_conv_b, lru_wa, lru_ba, lru_wx, lru_bx, lru_lambda, final_gain):
    depth = w_in.shape[0]
    d = x.shape[-1]
    stacked = (
        _modulation(c, w_mod, b_mod),
        norm_gain.reshape(depth, 1, d),
        _block_columns(w_in),
        w_out.astype(jnp.bfloat16),
        conv_a_w,
        sgu_w,
        jnp.broadcast_to(sgu_b[..., None], sgu_b.shape + (LANES,)),
        lru_conv_w,
        lru_conv_b.reshape(depth, 1, d),
        _pair_block_diag(lru_wa, lru_wx).astype(jnp.bfloat16),
        lru_ba.reshape(depth, 1, d),
        lru_bx.reshape(depth, 1, d),
        lru_lambda.reshape(depth, 1, d),
    )
    for l in range(depth):
        x = _layer(x, l, stacked, final_gain.reshape(1, d), final_norm=(l == depth - 1))
    return x
```

```python
import functools
import math

import jax
import jax.numpy as jnp
from jax import lax
from jax.experimental import pallas as pl
from jax.experimental.pallas import tpu as pltpu

D_MODEL = 1024
SGU_CHUNK = 128
SGU_HEADS = 8
LRU_HEADS = 16
LRU_HEAD_DIM = 64
CONV_K = 3
LRU_CONV_K = 4
LRU_C = 8.0
EPS = 1e-6
LOG2_E = math.log2(math.e)

SUBLANES = 8
LANES = 128
N_SLICES = 12
(A_X, A_B, A_C, A_Z, S_U, S_V, S_Z, R_X, R_Z, G_A, G_S, G_R) = range(N_SLICES)
N_COL_BLOCKS = D_MODEL // LANES
BLOCK_COLS = N_SLICES * LANES
SEQ_TILE = 256
ROW_CHUNK = 64
VMEM_LIMIT_BYTES = 56 * 1024 * 1024


def _sigmoid(v):
    return 1.0 / (1.0 + jnp.exp2(v * (-LOG2_E)))


def _silu(v):
    return v * _sigmoid(v)


def _mod_kernel(c_ref, w_ref, b_ref, o_ref):
    c = c_ref[...]
    o_ref[0, 0] = jnp.dot(_silu(c), w_ref[0], preferred_element_type=jnp.float32) + b_ref[0, 0]


def _modulation(c, w_mod, b_mod):
    depth, d, _ = w_mod.shape
    bsz = c.shape[0]
    return pl.pallas_call(
        _mod_kernel,
        grid=(depth, 3),
        in_specs=[
            pl.BlockSpec((bsz, d), lambda l, k: (0, 0)),
            pl.BlockSpec((1, d, d), lambda l, k: (l, 0, k)),
            pl.BlockSpec((1, 1, 1, d), lambda l, k: (l, k, 0, 0)),
        ],
        out_specs=pl.BlockSpec((1, 1, bsz, d), lambda l, k: (l, k, 0, 0)),
        out_shape=jax.ShapeDtypeStruct((depth, 3, bsz, d), jnp.float32),
        name="adaln_modulation",
    )(c, w_mod, b_mod.reshape(depth, 3, 1, d))


def _delay_rows(v3, prev_group, k, row_in_group):
    rolled = pltpu.roll(v3, k, axis=1)
    rolled_prev = pltpu.roll(prev_group, k, axis=0)
    before = jnp.concatenate([rolled_prev[None], rolled[:-1]], axis=0)
    return jnp.where(row_in_group >= k, rolled, before)


def _causal_conv(v3, prev_group, w_ref, col, row_in_group):
    k_taps = w_ref.shape[0]
    acc = w_ref[k_taps - 1:k_taps, col:col + LANES] * v3
    for j in range(k_taps - 1):
        acc = acc + w_ref[j:j + 1, col:col + LANES] * _delay_rows(
            v3, prev_group, k_taps - 1 - j, row_in_group)
    return acc


def _linear_scan(a3, b3, h, row_in_group):
    for k in (1, 2, 4):
        keep = row_in_group >= k
        a_prev = jnp.where(keep, pltpu.roll(a3, k, axis=1), 1.0)
        b_prev = jnp.where(keep, pltpu.roll(b3, k, axis=1), 0.0)
        b3 = a3 * b_prev + b3
        a3 = a3 * a_prev
    out = []
    for g in range(a3.shape[0]):
        hg = a3[g] * h + b3[g]
        out.append(hg)
        h = hg[SUBLANES - 1:SUBLANES, :]
    return jnp.stack(out, axis=0), h


def _layer_kernel(x_ref, mod_ref, gain_ref, w_in_ref, w_out_ref, conv_a_ref, sgu_w_ref,
                  sgu_b_ref, lru_cw_ref, lru_cb_ref, lru_w_ref, lru_ba_ref, lru_bx_ref,
                  lam_ref, fgain_ref, o_ref,
                  h_scr, m_scr, p_scr, xc_scr, xcb_scr, vn_scr, pre_scr, z_scr,
                  cx_carry, rx_carry, h_carry, *, final_norm):
    (mod_ref, gain_ref, w_in_ref, w_out_ref, conv_a_ref, sgu_w_ref, sgu_b_ref, lru_cw_ref,
     lru_cb_ref, lru_w_ref, lru_ba_ref, lru_bx_ref, lam_ref) = (
        r.at[0] for r in (mod_ref, gain_ref, w_in_ref, w_out_ref, conv_a_ref, sgu_w_ref,
                          sgu_b_ref, lru_cw_ref, lru_cb_ref, lru_w_ref, lru_ba_ref, lru_bx_ref,
                          lam_ref))
    tile = x_ref.shape[1]
    rows = ROW_CHUNK
    groups = rows // SUBLANES
    n_row_chunks = tile // rows
    n_sgu_chunks = tile // SGU_CHUNK
    b = pl.program_id(0)

    @pl.when(pl.program_id(1) == 0)
    def _():
        cx_carry[...] = jnp.zeros_like(cx_carry)
        rx_carry[...] = jnp.zeros_like(rx_carry)
        h_carry[...] = jnp.zeros_like(h_carry)

    shift = mod_ref[0, pl.ds(b, 1), :]
    scale = mod_ref[1, pl.ds(b, 1), :]
    gate = mod_ref[2, pl.ds(b, 1), :]

    gain_scale = gain_ref[...] * (1.0 + scale)
    for c in range(n_row_chunks):
        r0 = c * rows
        xs = x_ref[0, r0:r0 + rows, :]
        inv = lax.rsqrt(jnp.mean(xs * xs, axis=-1, keepdims=True) + EPS)
        h_scr[r0:r0 + rows, :] = ((xs * inv) * gain_scale + shift).astype(jnp.bfloat16)

    row_in_group = lax.broadcasted_iota(jnp.int32, (groups, SUBLANES, LANES), 1)
    causal = (lax.broadcasted_iota(jnp.int32, (SGU_CHUNK, SGU_CHUNK), 0)
              >= lax.broadcasted_iota(jnp.int32, (SGU_CHUNK, SGU_CHUNK), 1))

    def project(j):
        p_scr[j % 2] = jnp.dot(h_scr[...], w_in_ref[j], preferred_element_type=jnp.float32)

    def mix(j):
        col = j * LANES
        lanes = slice(col, col + LANES)
        p = p_scr.at[j % 2]
        xc_s, xcb_s, vn_s, pre_s, z_s = (
            r.at[j % 2] for r in (xc_scr, xcb_scr, vn_scr, pre_scr, z_scr))

        def sl(k, r0):
            return p[r0:r0 + rows, k * LANES:(k + 1) * LANES]

        def sl3(k, r0):
            return sl(k, r0).reshape(groups, SUBLANES, LANES)

        prev_rx = rx_carry[:, lanes]
        for c in range(n_row_chunks):
            r0 = c * rows
            rx3 = sl3(R_X, r0)
            xc = (_causal_conv(rx3, prev_rx, lru_cw_ref, col, row_in_group)
                  + lru_cb_ref[:, lanes]).reshape(rows, LANES)
            prev_rx = rx3[groups - 1]
            xc_s[r0:r0 + rows, :] = xc
            xcb_s[r0:r0 + rows, :] = xc.astype(jnp.bfloat16)
            v = sl(S_V, r0)
            vc = v - jnp.mean(v, axis=-1, keepdims=True)
            var = jnp.mean(vc * vc, axis=-1, keepdims=True)
            vn_s[r0:r0 + rows, :] = (vc * lax.rsqrt(var + EPS)).astype(jnp.bfloat16)
        rx_carry[:, lanes] = prev_rx

        pre_s[...] = jnp.dot(xcb_s[...], lru_w_ref[j], preferred_element_type=jnp.float32)
        vn_wide = jnp.concatenate(
            [vn_s[n * SGU_CHUNK:(n + 1) * SGU_CHUNK, :] for n in range(n_sgu_chunks)], axis=1)
        w_s = jnp.where(causal, sgu_w_ref[j], 0.0).astype(jnp.bfloat16)
        z_s[...] = jnp.dot(w_s, vn_wide, preferred_element_type=jnp.float32)

        lam = lam_ref[:, lanes]
        softplus_neg_lam = jnp.maximum(-lam, 0.0) + jnp.log1p(jnp.exp(-jnp.abs(lam)))
        log2_a_per_r = (-LRU_C * LOG2_E) * softplus_neg_lam
        prev_cx = cx_carry[:, lanes]
        state = h_carry[:, lanes]
        for c in range(n_row_chunks):
            r0 = c * rows
            chunk, q0 = divmod(r0, SGU_CHUNK)

            cx3 = sl3(A_C, r0) * sl3(A_X, r0)
            conv = _causal_conv(cx3, prev_cx, conv_a_ref, col, row_in_group).reshape(rows, LANES)
            prev_cx = cx3[groups - 1]
            merged = _sigmoid(sl(G_A, r0)) * ((_silu(sl(A_Z, r0)) * sl(A_B, r0)) * conv)

            z = (z_s[q0:q0 + rows, chunk * LANES:(chunk + 1) * LANES]
                 + sgu_b_ref[j, q0:q0 + rows, :])
            merged = merged + _sigmoid(sl(G_S, r0)) * ((_silu(sl(S_Z, r0)) * sl(S_U, r0)) * z)

            r = _sigmoid(pre_s[r0:r0 + rows, :LANES] + lru_ba_ref[:, lanes])
            i = _sigmoid(pre_s[r0:r0 + rows, LANES:] + lru_bx_ref[:, lanes])
            a = jnp.exp2(r * log2_a_per_r)
            y = 1.0 - a * a
            bb = (y * lax.rsqrt(jnp.maximum(y, 1e-30))) * (i * xc_s[r0:r0 + rows, :])
            hs3, state = _linear_scan(a.reshape(groups, SUBLANES, LANES),
                                      bb.reshape(groups, SUBLANES, LANES), state, row_in_group)
            merged = merged + _sigmoid(sl(G_R, r0)) * (
                _silu(sl(R_Z, r0)) * hs3.reshape(rows, LANES))

            m_scr[r0:r0 + rows, lanes] = merged.astype(jnp.bfloat16)
        cx_carry[:, lanes] = prev_cx
        h_carry[:, lanes] = state

    project(0)
    for j in range(N_COL_BLOCKS):
        if j + 1 < N_COL_BLOCKS:
            project(j + 1)
        mix(j)

    out = x_ref[0] + gate * jnp.dot(m_scr[...], w_out_ref[...], preferred_element_type=jnp.float32)
    if final_norm:
        inv_o = lax.rsqrt(jnp.mean(out * out, axis=-1, keepdims=True) + EPS)
        out = (out * inv_o) * fgain_ref[...]
    o_ref[0] = out


def _layer_resident(arr, layer):
    index = (layer,) + (0,) * (arr.ndim - 1)
    return pl.BlockSpec((1,) + arr.shape[1:], lambda b, s: index, pipeline_mode=pl.Buffered(1))


def _layer(x, layer, stacked, final_gain, *, final_norm):
    bsz, seq, d = x.shape
    tile = SEQ_TILE
    return pl.pallas_call(
        functools.partial(_layer_kernel, final_norm=final_norm),
        grid=(bsz, seq // tile),
        in_specs=[pl.BlockSpec((1, tile, d), lambda b, s: (b, s, 0))]
        + [_layer_resident(arr, layer) for arr in stacked]
        + [pl.BlockSpec(final_gain.shape, lambda b, s: (0, 0), pipeline_mode=pl.Buffered(1))],
        out_specs=pl.BlockSpec((1, tile, d), lambda b, s: (b, s, 0)),
        out_shape=jax.ShapeDtypeStruct(x.shape, x.dtype),
        scratch_shapes=[
            pltpu.VMEM((tile, d), jnp.bfloat16),
            pltpu.VMEM((tile, d), jnp.bfloat16),
            pltpu.VMEM((2, tile, BLOCK_COLS), jnp.float32),
            pltpu.VMEM((2, tile, LANES), jnp.float32),
            pltpu.VMEM((2, tile, LANES), jnp.bfloat16),
            pltpu.VMEM((2, tile, LANES), jnp.bfloat16),
            pltpu.VMEM((2, tile, 2 * LANES), jnp.float32),
            pltpu.VMEM((2, SGU_CHUNK, tile), jnp.float32),
            pltpu.VMEM((SUBLANES, d), jnp.float32),
            pltpu.VMEM((SUBLANES, d), jnp.float32),
            pltpu.VMEM((1, d), jnp.float32),
        ],
        compiler_params=pltpu.CompilerParams(
            dimension_semantics=("arbitrary", "arbitrary"),
            vmem_limit_bytes=VMEM_LIMIT_BYTES),
        name="hybrid_mixer_layer",
    )(x, *stacked, final_gain)


def _block_columns_kernel(*refs):
    o_ref = refs[-1]
    for k in range(N_SLICES):
        o_ref[0, 0, :, k * LANES:(k + 1) * LANES] = refs[k][0].astype(jnp.bfloat16)


def _slice_block_index(l, j, *, k):
    return (l, 0, k * N_COL_BLOCKS + j)


def _block_columns(w_in):
    depth, d, _ = w_in.shape
    return pl.pallas_call(
        _block_columns_kernel,
        grid=(depth, N_COL_BLOCKS),
        in_specs=[pl.BlockSpec((1, d, LANES), functools.partial(_slice_block_index, k=k))
                  for k in range(N_SLICES)],
        out_specs=pl.BlockSpec((1, 1, d, BLOCK_COLS), lambda l, j: (l, j, 0, 0)),
        out_shape=jax.ShapeDtypeStruct((depth, N_COL_BLOCKS, d, BLOCK_COLS), jnp.bfloat16),
        name="block_columns_bf16",
    )(*([w_in] * N_SLICES))


def _pair_block_diag(w_a, w_x):
    def pairs(w):
        w = w.reshape(w.shape[0], LRU_HEADS // 2, 2, LRU_HEAD_DIM, LRU_HEAD_DIM)
        zero = jnp.zeros_like(w[:, :, 0])
        top = jnp.concatenate([w[:, :, 0], zero], axis=-1)
        bottom = jnp.concatenate([zero, w[:, :, 1]], axis=-1)
        return jnp.concatenate([top, bottom], axis=-2)
    return jnp.concatenate([pairs(w_a), pairs(w_x)], axis=-1)


def kernel(x, c, norm_gain, w_mod, b_mod, w_in, w_out, conv_a_w, sgu_w, sgu_b, lru_conv_w,
           lru_conv_b, lru_wa, lru_ba, lru_wx, lru_bx, lru_lambda, final_gain):
    depth = w_in.shape[0]
    d = x.shape[-1]
    stacked = (
        _modulation(c, w_mod, b_mod),
        norm_gain.reshape(depth, 1, d),
        _block_columns(w_in),
        w_out.astype(jnp.bfloat16),
        conv_a_w,
        sgu_w,
        jnp.broadcast_to(sgu_b[..., None], sgu_b.shape + (LANES,)),
        lru_conv_w,
        lru_conv_b.reshape(depth, 1, d),
        _pair_block_diag(lru_wa, lru_wx).astype(jnp.bfloat16),
        lru_ba.reshape(depth, 1, d),
        lru_bx.reshape(depth, 1, d),
        lru_lambda.reshape(depth, 1, d),
    )
    for l in range(depth):
        x = _layer(x, l, stacked, final_gain.reshape(1, d), final_norm=(l == depth - 1))
    return x
```

```python
import functools
import math

import jax
import jax.numpy as jnp
from jax import lax
from jax.experimental import pallas as pl
from jax.experimental.pallas import tpu as pltpu

D_MODEL = 1024
SGU_CHUNK = 128
SGU_HEADS = 8
LRU_HEADS = 16
LRU_HEAD_DIM = 64
CONV_K = 3
LRU_CONV_K = 4
LRU_C = 8.0
EPS = 1e-6
LOG2_E = math.log2(math.e)

SUBLANES = 8
LANES = 128
N_SLICES = 12
(A_X, A_B, A_C, A_Z, S_U, S_V, S_Z, R_X, R_Z, G_A, G_S, G_R) = range(N_SLICES)
N_COL_BLOCKS = D_MODEL // LANES
BLOCK_COLS = N_SLICES * LANES
SEQ_TILE = 256
ROW_CHUNK = SUBLANES * SUBLANES
(PERM_RX, PERM_CX, PERM_CONV, PERM_H) = range(4)
N_PERM = 4
VMEM_LIMIT_BYTES = 56 * 1024 * 1024


def _sigmoid(v):
    return 1.0 / (1.0 + jnp.exp2(v * (-LOG2_E)))


def _silu(v):
    return v * _sigmoid(v)


def _mod_kernel(c_ref, w_ref, b_ref, o_ref):
    c = c_ref[...]
    o_ref[0, 0] = jnp.dot(_silu(c), w_ref[0], preferred_element_type=jnp.float32) + b_ref[0, 0]


def _modulation(c, w_mod, b_mod):
    depth, d, _ = w_mod.shape
    bsz = c.shape[0]
    return pl.pallas_call(
        _mod_kernel,
        grid=(depth, 3),
        in_specs=[
            pl.BlockSpec((bsz, d), lambda l, k: (0, 0)),
            pl.BlockSpec((1, d, d), lambda l, k: (l, 0, k)),
            pl.BlockSpec((1, 1, 1, d), lambda l, k: (l, k, 0, 0)),
        ],
        out_specs=pl.BlockSpec((1, 1, bsz, d), lambda l, k: (l, k, 0, 0)),
        out_shape=jax.ShapeDtypeStruct((depth, 3, bsz, d), jnp.float32),
        name="adaln_modulation",
    )(c, w_mod, b_mod.reshape(depth, 3, 1, d))


def _permute_rows(scr, r0, v3):
    for g in range(SUBLANES):
        scr[pl.ds(r0 + g, SUBLANES, stride=SUBLANES), :] = v3[g]
    return scr[r0:r0 + ROW_CHUNK, :].reshape(SUBLANES, SUBLANES, LANES)


def _causal_conv_permuted(vp3, prev_tail, w_ref, col):
    k_taps = w_ref.shape[0]
    n_prev = k_taps - 1
    tail = vp3[SUBLANES - n_prev:]
    sublane = lax.broadcasted_iota(jnp.int32, tail.shape, 1)
    before = jnp.where(sublane >= 1, pltpu.roll(tail, 1, axis=1), pltpu.roll(prev_tail, 1, axis=1))
    ext = jnp.concatenate([before, vp3], axis=0)
    acc = w_ref[k_taps - 1:k_taps, col:col + LANES] * vp3
    for j in range(n_prev):
        acc = acc + w_ref[j:j + 1, col:col + LANES] * ext[j:j + SUBLANES]
    return acc, tail


def _linear_scan_permuted(ap3, bp3, h):
    a_cum = [ap3[0]]
    b_cum = [bp3[0]]
    for r in range(1, SUBLANES):
        b_cum.append(ap3[r] * b_cum[-1] + bp3[r])
        a_cum.append(ap3[r] * a_cum[-1])
    alpha, beta = a_cum[-1], b_cum[-1]
    sublane = lax.broadcasted_iota(jnp.int32, alpha.shape, 0)
    for k in (1, 2, 4):
        keep = sublane >= k
        alpha_prev = jnp.where(keep, pltpu.roll(alpha, k, axis=0), 1.0)
        beta_prev = jnp.where(keep, pltpu.roll(beta, k, axis=0), 0.0)
        beta = alpha * beta_prev + beta
        alpha = alpha * alpha_prev
    after = alpha * h + beta
    entering = jnp.where(sublane >= 1, pltpu.roll(after, 1, axis=0), h)
    states = [a_cum[r] * entering + b_cum[r] for r in range(SUBLANES)]
    return jnp.stack(states, axis=0), after[SUBLANES - 1:SUBLANES, :]


def _layer_kernel(x_ref, x_next_ref, mod_ref, gain_ref, w_in_ref, w_out_ref, conv_a_ref, sgu_w_ref,
                  sgu_b_ref, lru_cw_ref, lru_cb_ref, lru_w_ref, lru_ba_ref, lru_bx_ref,
                  lam_ref, fgain_ref, o_ref,
                  h_scr, h_next_scr, m_scr, p_scr, xc_scr, xcb_scr, vn_scr, pre_scr, z_scr, perm_scr,
                  cx_carry, rx_carry, h_carry, *, final_norm, seq_tiles):
    (mod_ref, gain_ref, w_in_ref, w_out_ref, conv_a_ref, sgu_w_ref, sgu_b_ref, lru_cw_ref,
     lru_cb_ref, lru_w_ref, lru_ba_ref, lru_bx_ref, lam_ref) = (
        r.at[0] for r in (mod_ref, gain_ref, w_in_ref, w_out_ref, conv_a_ref, sgu_w_ref,
                          sgu_b_ref, lru_cw_ref, lru_cb_ref, lru_w_ref, lru_ba_ref, lru_bx_ref,
                          lam_ref))
    tile = x_ref.shape[1]
    rows = ROW_CHUNK
    n_row_chunks = tile // rows
    n_sgu_chunks = tile // SGU_CHUNK
    step = pl.program_id(0)
    b = step // seq_tiles
    b_next = jnp.minimum(step + 1, pl.num_programs(0) - 1) // seq_tiles

    @pl.when(step % seq_tiles == 0)
    def _():
        cx_carry[...] = jnp.zeros_like(cx_carry)
        rx_carry[...] = jnp.zeros_like(rx_carry)
        h_carry[...] = jnp.zeros_like(h_carry)

    def modulated_norm(src_ref, batch, dst_ref):
        shift = mod_ref[0, pl.ds(batch, 1), :]
        scale = mod_ref[1, pl.ds(batch, 1), :]
        gain_scale = gain_ref[...] * (1.0 + scale)
        for c in range(n_row_chunks):
            r0 = c * rows
            xs = src_ref[0, r0:r0 + rows, :]
            inv = lax.rsqrt(jnp.mean(xs * xs, axis=-1, keepdims=True) + EPS)
            dst_ref[r0:r0 + rows, :] = ((xs * inv) * gain_scale + shift).astype(jnp.bfloat16)

    def project(j, operand_ref):
        p_scr[j % 2] = jnp.dot(operand_ref[...], w_in_ref[j], preferred_element_type=jnp.float32)

    @pl.when(step == 0)
    def _():
        modulated_norm(x_ref, b, h_scr)
        project(0, h_scr)

    modulated_norm(x_next_ref, b_next, h_next_scr)
    gate = mod_ref[2, pl.ds(b, 1), :]

    causal = (lax.broadcasted_iota(jnp.int32, (SGU_CHUNK, SGU_CHUNK), 0)
              >= lax.broadcasted_iota(jnp.int32, (SGU_CHUNK, SGU_CHUNK), 1))

    def mix(j):
        col = j * LANES
        lanes = slice(col, col + LANES)
        p = p_scr.at[j % 2]
        xc_s, xcb_s, vn_s, pre_s, z_s, perm = (
            r.at[j % 2] for r in (xc_scr, xcb_scr, vn_scr, pre_scr, z_scr, perm_scr))

        def sl(k, r0):
            return p[r0:r0 + rows, k * LANES:(k + 1) * LANES]

        def sl3(k, r0):
            return sl(k, r0).reshape(SUBLANES, SUBLANES, LANES)

        rx_tail = rx_carry[:, :, lanes]
        for c in range(n_row_chunks):
            r0 = c * rows
            rxp = _permute_rows(perm.at[PERM_RX], r0, sl3(R_X, r0))
            xcp, rx_tail = _causal_conv_permuted(rxp, rx_tail, lru_cw_ref, col)
            xc = (xcp + lru_cb_ref[:, lanes]).reshape(rows, LANES)
            xc_s[r0:r0 + rows, :] = xc
            xcb_s[r0:r0 + rows, :] = xc.astype(jnp.bfloat16)
            v = sl(S_V, r0)
            vc = v - jnp.mean(v, axis=-1, keepdims=True)
            var = jnp.mean(vc * vc, axis=-1, keepdims=True)
            vn_s[r0:r0 + rows, :] = (vc * lax.rsqrt(var + EPS)).astype(jnp.bfloat16)
        rx_carry[:, :, lanes] = rx_tail

        pre_s[...] = jnp.dot(xcb_s[...], lru_w_ref[j], preferred_element_type=jnp.float32)
        vn_wide = jnp.concatenate(
            [vn_s[n * SGU_CHUNK:(n + 1) * SGU_CHUNK, :] for n in range(n_sgu_chunks)], axis=1)
        w_s = jnp.where(causal, sgu_w_ref[j], 0.0).astype(jnp.bfloat16)
        z_s[...] = jnp.dot(w_s, vn_wide, preferred_element_type=jnp.float32)

        lam = lam_ref[:, lanes]
        softplus_neg_lam = jnp.maximum(-lam, 0.0) + jnp.log1p(jnp.exp(-jnp.abs(lam)))
        log2_a_per_r = (-LRU_C * LOG2_E) * softplus_neg_lam
        cx_tail = cx_carry[:, :, lanes]
        state = h_carry[:, lanes]
        for c in range(n_row_chunks):
            r0 = c * rows
            chunk, q0 = divmod(r0, SGU_CHUNK)

            cxp = _permute_rows(perm.at[PERM_CX], r0, sl3(A_C, r0) * sl3(A_X, r0))
            convp, cx_tail = _causal_conv_permuted(cxp, cx_tail, conv_a_ref, col)
            conv = _permute_rows(perm.at[PERM_CONV], r0, convp).reshape(rows, LANES)
            merged = _sigmoid(sl(G_A, r0)) * ((_silu(sl(A_Z, r0)) * sl(A_B, r0)) * conv)

            z = (z_s[q0:q0 + rows, chunk * LANES:(chunk + 1) * LANES]
                 + sgu_b_ref[j, q0:q0 + rows, :])
            merged = merged + _sigmoid(sl(G_S, r0)) * ((_silu(sl(S_Z, r0)) * sl(S_U, r0)) * z)

            r =_sigmoid(pre_s[r0:r0 + rows, :LANES] + lru_ba_ref[:, lanes])
            i = _sigmoid(pre_s[r0:r0 + rows, LANES:] + lru_bx_ref[:, lanes])
            a = jnp.exp2(r * log2_a_per_r)
            y = 1.0 - a * a
            bb = (y * lax.rsqrt(jnp.maximum(y, 1e-30))) * (i * xc_s[r0:r0 + rows, :])
            hsp, state = _linear_scan_permuted(a.reshape(SUBLANES, SUBLANES, LANES),
                                               bb.reshape(SUBLANES, SUBLANES, LANES), state)
            hs = _permute_rows(perm.at[PERM_H], r0, hsp).reshape(rows, LANES)
            merged = merged + _sigmoid(sl(G_R, r0)) * (_silu(sl(R_Z, r0)) * hs)

            m_scr[r0:r0 + rows, lanes] = merged.astype(jnp.bfloat16)
        cx_carry[:, :, lanes] = cx_tail
        h_carry[:, lanes] = state

    for j in range(N_COL_BLOCKS):
        if j + 1 < N_COL_BLOCKS:
            project(j + 1, h_scr)
        else:
            project(0, h_next_scr)
        mix(j)

    out = x_ref[0] + gate * jnp.dot(m_scr[...], w_out_ref[...], preferred_element_type=jnp.float32)
    if final_norm:
        inv_o = lax.rsqrt(jnp.mean(out * out, axis=-1, keepdims=True) + EPS)
        out = (out * inv_o) * fgain_ref[...]
    o_ref[0] = out
    h_scr[...] = h_next_scr[...]


def _layer_resident(arr, layer):
    index = (layer,) + (0,) * (arr.ndim - 1)
    return pl.BlockSpec((1,) + arr.shape[1:], lambda i: index, pipeline_mode=pl.Buffered(1))


def _tile_index(i, *, seq_tiles, last, ahead):
    t = jnp.minimum(i + ahead, last)
    return (t // seq_tiles, t % seq_tiles, 0)


def _layer(x, layer, stacked, final_gain, *, final_norm):
    bsz, seq, d = x.shape
    tile = SEQ_TILE
    seq_tiles = seq // tile
    n_steps = bsz * seq_tiles
    tile_spec = functools.partial(_tile_index, seq_tiles=seq_tiles, last=n_steps - 1)
    return pl.pallas_call(
        functools.partial(_layer_kernel, final_norm=final_norm, seq_tiles=seq_tiles),
        grid=(n_steps,),
        in_specs=[pl.BlockSpec((1, tile, d), functools.partial(tile_spec, ahead=0)),
                  pl.BlockSpec((1, tile, d), functools.partial(tile_spec, ahead=1))]
        + [_layer_resident(arr, layer) for arr in stacked]
        + [pl.BlockSpec(final_gain.shape, lambda i: (0, 0), pipeline_mode=pl.Buffered(1))],
        out_specs=pl.BlockSpec((1, tile, d), functools.partial(tile_spec, ahead=0)),
        out_shape=jax.ShapeDtypeStruct(x.shape, x.dtype),
        scratch_shapes=[
            pltpu.VMEM((tile, d), jnp.bfloat16),
            pltpu.VMEM((tile, d), jnp.bfloat16),
            pltpu.VMEM((tile, d), jnp.bfloat16),
            pltpu.VMEM((2, tile, BLOCK_COLS), jnp.float32),
            pltpu.VMEM((2, tile, LANES), jnp.float32),
            pltpu.VMEM((2, tile, LANES), jnp.bfloat16),
            pltpu.VMEM((2, tile, LANES), jnp.bfloat16),
            pltpu.VMEM((2, tile, 2 * LANES), jnp.float32),
            pltpu.VMEM((2, SGU_CHUNK, tile), jnp.float32),
            pltpu.VMEM((2, N_PERM, tile, LANES), jnp.float32),
            pltpu.VMEM((CONV_K - 1, SUBLANES, d), jnp.float32),
            pltpu.VMEM((LRU_CONV_K - 1, SUBLANES, d), jnp.float32),
            pltpu.VMEM((1, d), jnp.float32),
        ],
        compiler_params=pltpu.CompilerParams(
            dimension_semantics=("arbitrary",),
            vmem_limit_bytes=VMEM_LIMIT_BYTES),
        name="hybrid_mixer_layer",
    )(x, x, *stacked, final_gain)


def _block_columns_kernel(*refs):
    o_ref = refs[-1]
    for k in range(N_SLICES):
        o_ref[0, 0, :, k * LANES:(k + 1) * LANES] = refs[k][0].astype(jnp.bfloat16)


def _slice_block_index(l, j, *, k):
    return (l, 0, k * N_COL_BLOCKS + j)


def _block_columns(w_in):
    depth, d, _ = w_in.shape
    return pl.pallas_call(
        _block_columns_kernel,
        grid=(depth, N_COL_BLOCKS),
        in_specs=[pl.BlockSpec((1, d, LANES), functools.partial(_slice_block_index, k=k))
                  for k in range(N_SLICES)],
        out_specs=pl.BlockSpec((1, 1, d, BLOCK_COLS), lambda l, j: (l, j, 0, 0)),
        out_shape=jax.ShapeDtypeStruct((depth, N_COL_BLOCKS, d, BLOCK_COLS), jnp.bfloat16),
        name="block_columns_bf16",
    )(*([w_in] * N_SLICES))


def _pair_block_diag(w_a, w_x):
    def pairs(w):
        w = w.reshape(w.shape[0], LRU_HEADS // 2, 2, LRU_HEAD_DIM, LRU_HEAD_DIM)
        zero = jnp.zeros_like(w[:, :, 0])
        top = jnp.concatenate([w[:, :, 0], zero], axis=-1)
        bottom = jnp.concatenate([zero, w[:, :, 1]], axis=-1)
        return jnp.concatenate([top, bottom], axis=-2)
    return jnp.concatenate([pairs(w_a), pairs(w_x)], axis=-1)


def kernel(x, c, norm_gain, w_mod, b_mod, w_in, w_out, conv_a_w, sgu_w, sgu_b, lru_conv_w,
           lru_conv_b, lru_wa, lru_ba, lru_wx, lru_bx, lru_lambda, final_gain):
    depth = w_in.shape[0]
    d = x.shape[-1]
    stacked = (
        _modulation(c, w_mod, b_mod),
        norm_gain.reshape(depth, 1, d),
        _block_columns(w_in),
        w_out.astype(jnp.bfloat16),
        conv_a_w,
        sgu_w,
        jnp.broadcast_to(sgu_b[..., None], sgu_b.shape + (LANES,)),
        lru_conv_w,
        lru_conv_b.reshape(depth, 1, d),
        _pair_block_diag(lru_wa, lru_wx).astype(jnp.bfloat16),
        lru_ba.reshape(depth, 1, d),
        lru_bx.reshape(depth, 1, d),
        lru_lambda.reshape(depth, 1, d),
    )
    for l in range(depth):
        x = _layer(x, l, stacked, final_gain.reshape(1, d), final_norm=(l == depth - 1))
    return x
```

```python
import functools
import math

import jax
import jax.numpy as jnp
from jax import lax
from jax.experimental import pallas as pl
from jax.experimental.pallas import tpu as pltpu

D_MODEL = 1024
SGU_CHUNK = 128
SGU_HEADS = 8
LRU_HEADS = 16
LRU_HEAD_DIM = 64
CONV_K = 3
LRU_CONV_K = 4
LRU_C = 8.0
EPS = 1e-6
LOG2_E = math.log2(math.e)

SUBLANES = 8
LANES = 128
N_SLICES = 12
(A_X, A_B, A_C, A_Z, S_U, S_V, S_Z, R_X, R_Z, G_A, G_S, G_R) = range(N_SLICES)
N_COL_BLOCKS = D_MODEL // LANES
BLOCK_COLS = N_SLICES * LANES
SEQ_TILE = 512
P_SLOTS = 2
ROW_CHUNK = SUBLANES * SUBLANES
(PERM_RX, PERM_CX, PERM_CONV, PERM_H) = range(4)
N_PERM = 4
VMEM_LIMIT_BYTES = 58 * 1024 * 1024


def _sigmoid(v):
    return 1.0 / (1.0 + jnp.exp2(v * (-LOG2_E)))


def _silu(v):
    return v * _sigmoid(v)


def _mod_kernel(c_ref, w_ref, b_ref, o_ref):
    c = c_ref[...]
    o_ref[0, 0] = jnp.dot(_silu(c), w_ref[0], preferred_element_type=jnp.float32) + b_ref[0, 0]


def _modulation(c, w_mod, b_mod):
    depth, d, _ = w_mod.shape
    bsz = c.shape[0]
    return pl.pallas_call(
        _mod_kernel,
        grid=(depth, 3),
        in_specs=[
            pl.BlockSpec((bsz, d), lambda l, k: (0, 0)),
            pl.BlockSpec((1, d, d), lambda l, k: (l, 0, k)),
            pl.BlockSpec((1, 1, 1, d), lambda l, k: (l, k, 0, 0)),
        ],
        out_specs=pl.BlockSpec((1, 1, bsz, d), lambda l, k: (l, k, 0, 0)),
        out_shape=jax.ShapeDtypeStruct((depth, 3, bsz, d), jnp.float32),
        name="adaln_modulation",
    )(c, w_mod, b_mod.reshape(depth, 3, 1, d))


def _permute_rows(scr, r0, v3):
    for g in range(SUBLANES):
        scr[pl.ds(r0 + g, SUBLANES, stride=SUBLANES), :] = v3[g]
    return scr[r0:r0 + ROW_CHUNK, :].reshape(SUBLANES, SUBLANES, LANES)


def _causal_conv_permuted(vp3, prev_tail, w_ref, col):
    k_taps = w_ref.shape[0]
    n_prev = k_taps - 1
    tail = vp3[SUBLANES - n_prev:]
    sublane = lax.broadcasted_iota(jnp.int32, tail.shape, 1)
    before = jnp.where(sublane >= 1, pltpu.roll(tail, 1, axis=1), pltpu.roll(prev_tail, 1, axis=1))
    ext = jnp.concatenate([before, vp3], axis=0)
    acc = w_ref[k_taps - 1:k_taps, col:col + LANES] * vp3
    for j in range(n_prev):
        acc = acc + w_ref[j:j + 1, col:col + LANES] * ext[j:j + SUBLANES]
    return acc, tail


def _linear_scan_permuted(ap3, bp3, h):
    a_cum = [ap3[0]]
    b_cum = [bp3[0]]
    for r in range(1, SUBLANES):
        b_cum.append(ap3[r] * b_cum[-1] + bp3[r])
        a_cum.append(ap3[r] * a_cum[-1])
    alpha, beta = a_cum[-1], b_cum[-1]
    sublane = lax.broadcasted_iota(jnp.int32, alpha.shape, 0)
    for k in (1, 2, 4):
        keep = sublane >= k
        alpha_prev = jnp.where(keep, pltpu.roll(alpha, k, axis=0), 1.0)
        beta_prev = jnp.where(keep, pltpu.roll(beta, k, axis=0), 0.0)
        beta = alpha * beta_prev + beta
        alpha = alpha * alpha_prev
    after = alpha * h + beta
    entering = jnp.where(sublane >= 1, pltpu.roll(after, 1, axis=0), h)
    states = [a_cum[r] * entering + b_cum[r] for r in range(SUBLANES)]
    return jnp.stack(states, axis=0), after[SUBLANES - 1:SUBLANES, :]


def _layer_kernel(*refs, final_norm, seq_tiles, prep_next):
    refs = list(refs)
    w_raw_ref = refs.pop(2) if prep_next else None
    w_next_ref = refs.pop(17) if prep_next else None
    (x_ref, x_next_ref, mod_ref, gain_ref, w_in_ref, w_out_ref, conv_a_ref, sgu_w_ref,
     sgu_b_ref, lru_cw_ref, lru_cb_ref, lru_w_ref, lru_ba_ref, lru_bx_ref,
     lam_ref, fgain_ref, o_ref,
     h_scr, h_next_scr, m_scr, p_scr, xc_scr, xcb_scr, vn_scr, pre_scr, z_scr, perm_scr,
     cx_carry, rx_carry, h_carry) = refs
    (mod_ref, gain_ref, w_in_ref, w_out_ref, conv_a_ref, sgu_w_ref, sgu_b_ref, lru_cw_ref,
     lru_cb_ref, lru_w_ref, lru_ba_ref, lru_bx_ref, lam_ref) = (
        r.at[0] for r in (mod_ref, gain_ref, w_in_ref, w_out_ref, conv_a_ref, sgu_w_ref,
                          sgu_b_ref, lru_cw_ref, lru_cb_ref, lru_w_ref, lru_ba_ref, lru_bx_ref,
                          lam_ref))
    tile = x_ref.shape[1]
    rows = ROW_CHUNK
    n_row_chunks = tile // rows
    n_sgu_chunks = tile // SGU_CHUNK
    step = pl.program_id(0)
    b = step // seq_tiles
    b_next = jnp.minimum(step + 1, pl.num_programs(0) - 1) // seq_tiles

    @pl.when(step % seq_tiles == 0)
    def _():
        cx_carry[...] = jnp.zeros_like(cx_carry)
        rx_carry[...] = jnp.zeros_like(rx_carry)
        h_carry[...] = jnp.zeros_like(h_carry)

    if prep_next:
        for jb in range(N_COL_BLOCKS):
            for k in range(N_SLICES):
                c0 = k * D_MODEL + jb * LANES
                w_next_ref[jb, :, k * LANES:(k + 1) * LANES] = (
                    w_raw_ref[0, :, c0:c0 + LANES].astype(jnp.bfloat16))

    def modulated_norm(src_ref, batch, dst_ref):
        shift = mod_ref[0, pl.ds(batch, 1), :]
        scale = mod_ref[1, pl.ds(batch, 1), :]
        gain_scale = gain_ref[...] * (1.0 + scale)
        for c in range(n_row_chunks):
            r0 = c * rows
            xs = src_ref[0, r0:r0 + rows, :]
            inv = lax.rsqrt(jnp.mean(xs * xs, axis=-1, keepdims=True) + EPS)
            dst_ref[r0:r0 + rows, :] = ((xs * inv) * gain_scale + shift).astype(jnp.bfloat16)

    def project(j, operand_ref, half):
        cols = slice(half * (BLOCK_COLS // 2), (half + 1) * (BLOCK_COLS // 2))
        p_scr[j % P_SLOTS, :, cols] = jnp.dot(operand_ref[...], w_in_ref[j, :, cols],
                                              preferred_element_type=jnp.float32)

    @pl.when(step == 0)
    def _():
        modulated_norm(x_ref, b, h_scr)
        project(0, h_scr, 0)
        project(0, h_scr, 1)

    modulated_norm(x_next_ref, b_next, h_next_scr)
    gate = mod_ref[2, pl.ds(b, 1), :]

    causal = (lax.broadcasted_iota(jnp.int32, (SGU_CHUNK, SGU_CHUNK), 0)
              >= lax.broadcasted_iota(jnp.int32, (SGU_CHUNK, SGU_CHUNK), 1))

    def mix(j):
        col = j * LANES
        lanes = slice(col, col + LANES)
        p = p_scr.at[j % P_SLOTS]
        xc_s, xcb_s, vn_s, pre_s, z_s, perm = (
            r.at[j % 2] for r in (xc_scr, xcb_scr, vn_scr, pre_scr, z_scr, perm_scr))

        def sl(k, r0):
            return p[r0:r0 + rows, k * LANES:(k + 1) * LANES]

        def sl3(k, r0):
            return sl(k, r0).reshape(SUBLANES, SUBLANES, LANES)

        rx_tail = rx_carry[:, :, lanes]
        for c in range(n_row_chunks):
            r0 = c * rows
            rxp = _permute_rows(perm.at[PERM_RX], r0, sl3(R_X, r0))
            xcp, rx_tail = _causal_conv_permuted(rxp, rx_tail, lru_cw_ref, col)
            xc = (xcp + lru_cb_ref[:, lanes]).reshape(rows, LANES)
            xc_s[r0:r0 + rows, :] = xc
            xcb_s[r0:r0 + rows, :] = xc.astype(jnp.bfloat16)
            v = sl(S_V, r0)
            vc = v - jnp.mean(v, axis=-1, keepdims=True)
            var = jnp.mean(vc * vc, axis=-1, keepdims=True)
            vn_s[r0:r0 + rows, :] = (vc * lax.rsqrt(var + EPS)).astype(jnp.bfloat16)
        rx_carry[:, :, lanes] = rx_tail

        pre_s[...] = jnp.dot(xcb_s[...], lru_w_ref[j], preferred_element_type=jnp.float32)
        vn_wide = jnp.concatenate(
            [vn_s[n * SGU_CHUNK:(n + 1) * SGU_CHUNK, :] for n in range(n_sgu_chunks)], axis=1)
        w_s = jnp.where(causal, sgu_w_ref[j], 0.0).astype(jnp.bfloat16)
        z_s[...] = jnp.dot(w_s, vn_wide, preferred_element_type=jnp.float32)
        yield

        lam = lam_ref[:, lanes]
        softplus_neg_lam = jnp.maximum(-lam, 0.0) + jnp.log1p(jnp.exp(-jnp.abs(lam)))
        log2_a_per_r = (-LRU_C * LOG2_E) * softplus_neg_lam
        cx_tail = cx_carry[:, :, lanes]
        state = h_carry[:, lanes]
        for c in range(n_row_chunks):
            r0 = c * rows
            chunk, q0 = divmod(r0, SGU_CHUNK)

            cxp = _permute_rows(perm.at[PERM_CX], r0, sl3(A_C, r0) * sl3(A_X, r0))
            convp, cx_tail = _causal_conv_permuted(cxp, cx_tail, conv_a_ref, col)
            conv = _permute_rows(perm.at[PERM_CONV], r0, convp).reshape(rows, LANES)
            merged = _sigmoid(sl(G_A, r0)) * ((_silu(sl(A_Z, r0)) * sl(A_B, r0)) * conv)

            z = (z_s[q0:q0 + rows, chunk * LANES:(chunk + 1) * LANES]
                 + sgu_b_ref[j, q0:q0 + rows, :])
            merged = merged + _sigmoid(sl(G_S, r0)) * ((_silu(sl(S_Z, r0)) * sl(S_U, r0)) * z)

            r =_sigmoid(pre_s[r0:r0 + rows, :LANES] + lru_ba_ref[:, lanes])
            i = _sigmoid(pre_s[r0:r0 + rows, LANES:] + lru_bx_ref[:, lanes])
            a = jnp.exp2(r * log2_a_per_r)
            y = 1.0 - a * a
            bb = (y * lax.rsqrt(jnp.maximum(y, 1e-30))) * (i * xc_s[r0:r0 + rows, :])
            hsp, state = _linear_scan_permuted(a.reshape(SUBLANES, SUBLANES, LANES),
                                               bb.reshape(SUBLANES, SUBLANES, LANES), state)
            hs = _permute_rows(perm.at[PERM_H], r0, hsp).reshape(rows, LANES)
            merged = merged + _sigmoid(sl(G_R, r0)) * (_silu(sl(R_Z, r0)) * hs)

            m_scr[r0:r0 + rows, lanes] = merged.astype(jnp.bfloat16)
        cx_carry[:, :, lanes] = cx_tail
        h_carry[:, lanes] = state

    for j in range(N_COL_BLOCKS):
        upcoming = (j + 1, h_scr) if j + 1 < N_COL_BLOCKS else (0, h_next_scr)
        mix_parts = mix(j)
        project(*upcoming, 0)
        next(mix_parts)
        project(*upcoming, 1)
        next(mix_parts, None)

    out = x_ref[0] + gate * jnp.dot(m_scr[...], w_out_ref[...], preferred_element_type=jnp.float32)
    if final_norm:
        inv_o = lax.rsqrt(jnp.mean(out * out, axis=-1, keepdims=True) + EPS)
        out = (out * inv_o) * fgain_ref[...]
    o_ref[0] = out
    h_scr[...] = h_next_scr[...]


def _layer_resident(arr, layer):
    index = (layer,) + (0,) * (arr.ndim - 1)
    return pl.BlockSpec((1,) + arr.shape[1:], lambda i: index, pipeline_mode=pl.Buffered(1))


def _tile_index(i, *, seq_tiles, last, ahead):
    t = jnp.minimum(i + ahead, last)
    return (t // seq_tiles, t % seq_tiles, 0)


def _layer(x, layer, stacked, w_in_blocked, w_in_raw, final_gain, *, final_norm):
    bsz, seq, d = x.shape
    tile = SEQ_TILE
    seq_tiles = seq // tile
    n_steps = bsz * seq_tiles
    prep_next = layer + 1 < w_in_raw.shape[0]
    slab = d // n_steps
    tile_spec = functools.partial(_tile_index, seq_tiles=seq_tiles, last=n_steps - 1)
    operands = [(arr, layer) for arr in stacked]
    operands.insert(2, (w_in_blocked[None], 0))
    in_specs = [pl.BlockSpec((1, tile, d), functools.partial(tile_spec, ahead=0)),
                pl.BlockSpec((1, tile, d), functools.partial(tile_spec, ahead=1))]
    inputs = [x, x]
    out_specs = [pl.BlockSpec((1, tile, d), functools.partial(tile_spec, ahead=0))]
    out_shape = [jax.ShapeDtypeStruct(x.shape, x.dtype)]
    if prep_next:
        in_specs.append(pl.BlockSpec((1, slab, w_in_raw.shape[2]), lambda i: (layer + 1, i, 0)))
        inputs.append(w_in_raw)
        out_specs.append(pl.BlockSpec((N_COL_BLOCKS, slab, BLOCK_COLS), lambda i: (0, i, 0)))
        out_shape.append(jax.ShapeDtypeStruct((N_COL_BLOCKS, d, BLOCK_COLS), jnp.bfloat16))
    in_specs += [_layer_resident(arr, index) for arr, index in operands]
    in_specs.append(pl.BlockSpec(final_gain.shape, lambda i: (0, 0), pipeline_mode=pl.Buffered(1)))
    inputs += [arr for arr, _ in operands] + [final_gain]
    outs = pl.pallas_call(
        functools.partial(_layer_kernel, final_norm=final_norm, seq_tiles=seq_tiles,
                          prep_next=prep_next),
        grid=(n_steps,),
        in_specs=in_specs,
        out_specs=out_specs,
        out_shape=out_shape,
        scratch_shapes=[
            pltpu.VMEM((tile, d), jnp.bfloat16),
            pltpu.VMEM((tile, d), jnp.bfloat16),
            pltpu.VMEM((tile, d), jnp.bfloat16),
            pltpu.VMEM((P_SLOTS, tile, BLOCK_COLS), jnp.float32),
            pltpu.VMEM((2, tile, LANES), jnp.float32),
            pltpu.VMEM((2, tile, LANES), jnp.bfloat16),
            pltpu.VMEM((2, tile, LANES), jnp.bfloat16),
            pltpu.VMEM((2, tile, 2 * LANES), jnp.float32),
            pltpu.VMEM((2, SGU_CHUNK, tile), jnp.float32),
            pltpu.VMEM((2, N_PERM, tile, LANES), jnp.float32),
            pltpu.VMEM((CONV_K - 1, SUBLANES, d), jnp.float32),
            pltpu.VMEM((LRU_CONV_K - 1, SUBLANES, d), jnp.float32),
            pltpu.VMEM((1, d), jnp.float32),
        ],
        compiler_params=pltpu.CompilerParams(
            dimension_semantics=("arbitrary",),
            vmem_limit_bytes=VMEM_LIMIT_BYTES),
        name="hybrid_mixer_layer",
    )(*inputs)
    return (outs[0], outs[1]) if prep_next else (outs[0], None)


def _block_columns_kernel(*refs):
    o_ref = refs[-1]
    for k in range(N_SLICES):
        o_ref[0, :, k * LANES:(k + 1) * LANES] = refs[k][0].astype(jnp.bfloat16)


def _slice_block_index(j, *, layer, k):
    return (layer, 0, k * N_COL_BLOCKS + j)


def _block_columns(w_in, layer):
    _, d, _ = w_in.shape
    return pl.pallas_call(
        _block_columns_kernel,
        grid=(N_COL_BLOCKS,),
        in_specs=[pl.BlockSpec((1, d, LANES), functools.partial(_slice_block_index, layer=layer, k=k))
                  for k in range(N_SLICES)],
        out_specs=pl.BlockSpec((1, d, BLOCK_COLS), lambda j: (j, 0, 0)),
        out_shape=jax.ShapeDtypeStruct((N_COL_BLOCKS, d, BLOCK_COLS), jnp.bfloat16),
        name="block_columns_bf16",
    )(*([w_in] * N_SLICES))


def _pair_block_diag(w_a, w_x):
    def pairs(w):
        w = w.reshape(w.shape[0], LRU_HEADS // 2, 2, LRU_HEAD_DIM, LRU_HEAD_DIM)
        zero = jnp.zeros_like(w[:, :, 0])
        top = jnp.concatenate([w[:, :, 0], zero], axis=-1)
        bottom = jnp.concatenate([zero, w[:, :, 1]], axis=-1)
        return jnp.concatenate([top, bottom], axis=-2)
    return jnp.concatenate([pairs(w_a), pairs(w_x)], axis=-1)


def kernel(x, c, norm_gain, w_mod, b_mod, w_in, w_out, conv_a_w, sgu_w, sgu_b, lru_conv_w,
           lru_conv_b, lru_wa, lru_ba, lru_wx, lru_bx, lru_lambda, final_gain):
    depth = w_in.shape[0]
    d = x.shape[-1]
    stacked = (
        _modulation(c, w_mod, b_mod),
        norm_gain.reshape(depth, 1, d),
        w_out.astype(jnp.bfloat16),
        conv_a_w,
        sgu_w,
        jnp.broadcast_to(sgu_b[..., None], sgu_b.shape + (LANES,)),
        lru_conv_w,
        lru_conv_b.reshape(depth, 1, d),
        _pair_block_diag(lru_wa, lru_wx).astype(jnp.bfloat16),
        lru_ba.reshape(depth, 1, d),
        lru_bx.reshape(depth, 1, d),
        lru_lambda.reshape(depth, 1, d),
    )
    w_in_blocked = _block_columns(w_in, 0)
    for l in range(depth):
        x, w_in_blocked = _layer(x, l, stacked, w_in_blocked, w_in, final_gain.reshape(1, d),
                                 final_norm=(l == depth - 1))
    return x
```

```python
import functools
import math

import jax
import jax.numpy as jnp
from jax import lax
from jax.experimental import pallas as pl
from jax.experimental.pallas import tpu as pltpu

D_MODEL = 1024
SGU_CHUNK = 128
SGU_HEADS = 8
LRU_HEADS = 16
LRU_HEAD_DIM = 64
CONV_K = 3
LRU_CONV_K = 4
LRU_C = 8.0
EPS = 1e-6
LOG2_E = math.log2(math.e)

SUBLANES = 8
LANES = 128
N_SLICES = 12
(A_X, A_B, A_C, A_Z, S_U, S_V, S_Z, R_X, R_Z, G_A, G_S, G_R) = range(N_SLICES)
N_COL_BLOCKS = D_MODEL // LANES
BLOCK_COLS = N_SLICES * LANES
SEQ_TILE = 512
P_SLOTS = 2
ROW_CHUNK = SUBLANES * SUBLANES
PERM_PITCH = SUBLANES + 1
PERM_ROWS = SUBLANES * PERM_PITCH
VMEM_LIMIT_BYTES = 58 * 1024 * 1024


def _sigmoid(v):
    return 1.0 / (1.0 + jnp.exp2(v * (-LOG2_E)))


def _silu(v):
    return v * _sigmoid(v)


def _mod_kernel(c_ref, w_ref, b_ref, o_ref):
    c = c_ref[...]
    o_ref[0, 0] = jnp.dot(_silu(c), w_ref[0], preferred_element_type=jnp.float32) + b_ref[0, 0]


def _modulation(c, w_mod, b_mod):
    depth, d, _ = w_mod.shape
    bsz = c.shape[0]
    return pl.pallas_call(
        _mod_kernel,
        grid=(depth, 3),
        in_specs=[
            pl.BlockSpec((bsz, d), lambda l, k: (0, 0)),
            pl.BlockSpec((1, d, d), lambda l, k: (l, 0, k)),
            pl.BlockSpec((1, 1, 1, d), lambda l, k: (l, k, 0, 0)),
        ],
        out_specs=pl.BlockSpec((1, 1, bsz, d), lambda l, k: (l, k, 0, 0)),
        out_shape=jax.ShapeDtypeStruct((depth, 3, bsz, d), jnp.float32),
        name="adaln_modulation",
    )(c, w_mod, b_mod.reshape(depth, 3, 1, d))


def _permute_rows(scr, chunk, v3, dyn_zero):
    base = chunk * PERM_ROWS
    for g in range(SUBLANES):
        scr[pl.ds(base + g, SUBLANES, stride=PERM_PITCH), :] = v3[g]
    return jnp.stack([scr[pl.ds(dyn_zero + (base + PERM_PITCH * r), SUBLANES), :]
                      for r in range(SUBLANES)], axis=0)


def _causal_conv_permuted(vp3, prev_tail, w_ref, col):
    k_taps = w_ref.shape[0]
    n_prev = k_taps - 1
    tail = vp3[SUBLANES - n_prev:]
    sublane = lax.broadcasted_iota(jnp.int32, tail.shape, 1)
    before = jnp.where(sublane >= 1, pltpu.roll(tail, 1, axis=1), pltpu.roll(prev_tail, 1, axis=1))
    ext = jnp.concatenate([before, vp3], axis=0)
    acc = w_ref[k_taps - 1:k_taps, col:col + LANES] * vp3
    for j in range(n_prev):
        acc = acc + w_ref[j:j + 1, col:col + LANES] * ext[j:j + SUBLANES]
    return acc, tail


def _linear_scan_permuted(ap3, bp3, h):
    a_cum = [ap3[0]]
    b_cum = [bp3[0]]
    for r in range(1, SUBLANES):
        b_cum.append(ap3[r] * b_cum[-1] + bp3[r])
        a_cum.append(ap3[r] * a_cum[-1])
    alpha, beta = a_cum[-1], b_cum[-1]
    sublane = lax.broadcasted_iota(jnp.int32, alpha.shape, 0)
    for k in (1, 2, 4):
        keep = sublane >= k
        alpha_prev = jnp.where(keep, pltpu.roll(alpha, k, axis=0), 1.0)
        beta_prev = jnp.where(keep, pltpu.roll(beta, k, axis=0), 0.0)
        beta = alpha * beta_prev + beta
        alpha = alpha * alpha_prev
    after = alpha * h + beta
    entering = jnp.where(sublane >= 1, pltpu.roll(after, 1, axis=0), h)
    states = [a_cum[r] * entering + b_cum[r] for r in range(SUBLANES)]
    return jnp.stack(states, axis=0), after[SUBLANES - 1:SUBLANES, :]


def _layer_kernel(*refs, final_norm, seq_tiles, prep_next):
    refs = list(refs)
    w_raw_ref = refs.pop(2) if prep_next else None
    w_next_ref = refs.pop(17) if prep_next else None
    (x_ref, x_next_ref, mod_ref, gain_ref, w_in_ref, w_out_ref, conv_a_ref, sgu_w_ref,
     sgu_b_ref, lru_cw_ref, lru_cb_ref, lru_w_ref, lru_ba_ref, lru_bx_ref,
     lam_ref, fgain_ref, o_ref,
     h_scr, h_next_scr, m_scr, p_scr, xc_scr, xcb_scr, vn_scr, pre_scr, z_scr,
     perm_rx_scr, perm_cx_scr, perm_conv_scr, perm_h_scr,
     cx_carry, rx_carry, h_carry) = refs
    (mod_ref, gain_ref, w_in_ref, w_out_ref, conv_a_ref, sgu_w_ref, sgu_b_ref, lru_cw_ref,
     lru_cb_ref, lru_w_ref, lru_ba_ref, lru_bx_ref, lam_ref) = (
        r.at[0] for r in (mod_ref, gain_ref, w_in_ref, w_out_ref, conv_a_ref, sgu_w_ref,
                          sgu_b_ref, lru_cw_ref, lru_cb_ref, lru_w_ref, lru_ba_ref, lru_bx_ref,
                          lam_ref))
    tile = x_ref.shape[1]
    rows = ROW_CHUNK
    n_row_chunks = tile // rows
    n_sgu_chunks = tile // SGU_CHUNK
    step = pl.program_id(0)
    dyn_zero = step // pl.num_programs(0)
    b = step // seq_tiles
    b_next = jnp.minimum(step + 1, pl.num_programs(0) - 1) // seq_tiles

    @pl.when(step % seq_tiles == 0)
    def _():
        cx_carry[...] = jnp.zeros_like(cx_carry)
        rx_carry[...] = jnp.zeros_like(rx_carry)
        h_carry[...] = jnp.zeros_like(h_carry)

    if prep_next:
        for jb in range(N_COL_BLOCKS):
            for k in range(N_SLICES):
                c0 = k * D_MODEL + jb * LANES
                w_next_ref[jb, :, k * LANES:(k + 1) * LANES] = (
                    w_raw_ref[0, :, c0:c0 + LANES].astype(jnp.bfloat16))

    def modulated_norm(src_ref, batch, dst_ref):
        shift = mod_ref[0, pl.ds(batch, 1), :]
        scale = mod_ref[1, pl.ds(batch, 1), :]
        gain_scale = gain_ref[...] * (1.0 + scale)
        for c in range(n_row_chunks):
            r0 = c * rows
            xs = src_ref[0, r0:r0 + rows, :]
            inv = lax.rsqrt(jnp.mean(xs * xs, axis=-1, keepdims=True) + EPS)
            dst_ref[r0:r0 + rows, :] = ((xs * inv) * gain_scale + shift).astype(jnp.bfloat16)

    def project(j, operand_ref, half):
        cols = slice(half * (BLOCK_COLS // 2), (half + 1) * (BLOCK_COLS // 2))
        p_scr[j % P_SLOTS, :, cols] = jnp.dot(operand_ref[...], w_in_ref[j, :, cols],
                                              preferred_element_type=jnp.float32)

    @pl.when(step == 0)
    def _():
        modulated_norm(x_ref, b, h_scr)
        project(0, h_scr, 0)
        project(0, h_scr, 1)

    modulated_norm(x_next_ref, b_next, h_next_scr)
    gate = mod_ref[2, pl.ds(b, 1), :]

    causal = (lax.broadcasted_iota(jnp.int32, (SGU_CHUNK, SGU_CHUNK), 0)
              >= lax.broadcasted_iota(jnp.int32, (SGU_CHUNK, SGU_CHUNK), 1))

    def mix(j):
        col = j * LANES
        lanes = slice(col, col + LANES)
        p = p_scr.at[j % P_SLOTS]
        xc_s, xcb_s, vn_s, pre_s, z_s, perm_rx, perm_cx, perm_conv, perm_h = (
            r.at[j % 2] for r in (xc_scr, xcb_scr, vn_scr, pre_scr, z_scr,
                                  perm_rx_scr, perm_cx_scr, perm_conv_scr, perm_h_scr))

        def sl(k, r0):
            return p[r0:r0 + rows, k * LANES:(k + 1) * LANES]

        def sl3(k, r0):
            return sl(k, r0).reshape(SUBLANES, SUBLANES, LANES)

        rx_tail = rx_carry[:, :, lanes]
        for c in range(n_row_chunks):
            r0 = c * rows
            rxp = _permute_rows(perm_rx, c, sl3(R_X, r0), dyn_zero)
            xcp, rx_tail = _causal_conv_permuted(rxp, rx_tail, lru_cw_ref, col)
            xc = (xcp + lru_cb_ref[:, lanes]).reshape(rows, LANES)
            xc_s[r0:r0 + rows, :] = xc
            xcb_s[r0:r0 + rows, :] = xc.astype(jnp.bfloat16)
            v = sl(S_V, r0)
            vc = v - jnp.mean(v, axis=-1, keepdims=True)
            var = jnp.mean(vc * vc, axis=-1, keepdims=True)
            vn_s[r0:r0 + rows, :] = (vc * lax.rsqrt(var + EPS)).astype(jnp.bfloat16)
        rx_carry[:, :, lanes] = rx_tail

        pre_s[...] = jnp.dot(xcb_s[...], lru_w_ref[j], preferred_element_type=jnp.float32)
        vn_wide = jnp.concatenate(
            [vn_s[n * SGU_CHUNK:(n + 1) * SGU_CHUNK, :] for n in range(n_sgu_chunks)], axis=1)
        w_s = jnp.where(causal, sgu_w_ref[j], 0.0).astype(jnp.bfloat16)
        z_s[...] = jnp.dot(w_s, vn_wide, preferred_element_type=jnp.float32)
        yield

        lam = lam_ref[:, lanes]
        softplus_neg_lam = jnp.maximum(-lam, 0.0) + jnp.log1p(jnp.exp(-jnp.abs(lam)))
        log2_a_per_r = (-LRU_C * LOG2_E) * softplus_neg_lam
        cx_tail = cx_carry[:, :, lanes]
        state = h_carry[:, lanes]
        for c in range(n_row_chunks):
            r0 = c * rows
            chunk, q0 = divmod(r0, SGU_CHUNK)

            cxp = _permute_rows(perm_cx, c, sl3(A_C, r0) * sl3(A_X, r0), dyn_zero)
            convp, cx_tail = _causal_conv_permuted(cxp, cx_tail, conv_a_ref, col)
            conv = _permute_rows(perm_conv, c, convp, dyn_zero).reshape(rows, LANES)
            merged = _sigmoid(sl(G_A, r0)) * ((_silu(sl(A_Z, r0)) * sl(A_B, r0)) * conv)

            z = (z_s[q0:q0 + rows, chunk * LANES:(chunk + 1) * LANES]
                 + sgu_b_ref[j, q0:q0 + rows, :])
            merged = merged + _sigmoid(sl(G_S, r0)) * ((_silu(sl(S_Z, r0)) * sl(S_U, r0)) * z)

            r =_sigmoid(pre_s[r0:r0 + rows, :LANES] + lru_ba_ref[:, lanes])
            i = _sigmoid(pre_s[r0:r0 + rows, LANES:] + lru_bx_ref[:, lanes])
            a = jnp.exp2(r * log2_a_per_r)
            y = 1.0 - a * a
            bb = (y * lax.rsqrt(jnp.maximum(y, 1e-30))) * (i * xc_s[r0:r0 + rows, :])
            hsp, state = _linear_scan_permuted(a.reshape(SUBLANES, SUBLANES, LANES),
                                               bb.reshape(SUBLANES, SUBLANES, LANES), state)
            hs = _permute_rows(perm_h, c, hsp, dyn_zero).reshape(rows, LANES)
            merged = merged + _sigmoid(sl(G_R, r0)) * (_silu(sl(R_Z, r0)) * hs)

            m_scr[r0:r0 + rows, lanes] = merged.astype(jnp.bfloat16)
        cx_carry[:, :, lanes] = cx_tail
        h_carry[:, lanes] = state

    for j in range(N_COL_BLOCKS):
        upcoming = (j + 1, h_scr) if j + 1 < N_COL_BLOCKS else (0, h_next_scr)
        mix_parts = mix(j)
        project(*upcoming, 0)
        next(mix_parts)
        project(*upcoming, 1)
        next(mix_parts, None)

    out = x_ref[0] + gate * jnp.dot(m_scr[...], w_out_ref[...], preferred_element_type=jnp.float32)
    if final_norm:
        inv_o = lax.rsqrt(jnp.mean(out * out, axis=-1, keepdims=True) + EPS)
        out = (out * inv_o) * fgain_ref[...]
    o_ref[0] = out
    h_scr[...] = h_next_scr[...]


def _layer_resident(arr, layer):
    index = (layer,) + (0,) * (arr.ndim - 1)
    return pl.BlockSpec((1,) + arr.shape[1:], lambda i: index, pipeline_mode=pl.Buffered(1))


def _tile_index(i, *, seq_tiles, last, ahead):
    t = jnp.minimum(i + ahead, last)
    return (t // seq_tiles, t % seq_tiles, 0)


def _layer(x, layer, stacked, w_in_blocked, w_in_raw, final_gain, *, final_norm):
    bsz, seq, d = x.shape
    tile = SEQ_TILE
    seq_tiles = seq // tile
    n_steps = bsz * seq_tiles
    prep_next = layer + 1 < w_in_raw.shape[0]
    slab = d // n_steps
    tile_spec = functools.partial(_tile_index, seq_tiles=seq_tiles, last=n_steps - 1)
    perm_staging = pltpu.VMEM((2, tile // ROW_CHUNK * PERM_ROWS, LANES), jnp.float32)
    operands =[(arr, layer) for arr in stacked]
    operands.insert(2, (w_in_blocked[None], 0))
    in_specs = [pl.BlockSpec((1, tile, d), functools.partial(tile_spec, ahead=0)),
                pl.BlockSpec((1, tile, d), functools.partial(tile_spec, ahead=1))]
    inputs = [x, x]
    out_specs = [pl.BlockSpec((1, tile, d), functools.partial(tile_spec, ahead=0))]
    out_shape = [jax.ShapeDtypeStruct(x.shape, x.dtype)]
    if prep_next:
        in_specs.append(pl.BlockSpec((1, slab, w_in_raw.shape[2]), lambda i: (layer + 1, i, 0)))
        inputs.append(w_in_raw)
        out_specs.append(pl.BlockSpec((N_COL_BLOCKS, slab, BLOCK_COLS), lambda i: (0, i, 0)))
        out_shape.append(jax.ShapeDtypeStruct((N_COL_BLOCKS, d, BLOCK_COLS), jnp.bfloat16))
    in_specs += [_layer_resident(arr, index) for arr, index in operands]
    in_specs.append(pl.BlockSpec(final_gain.shape, lambda i: (0, 0), pipeline_mode=pl.Buffered(1)))
    inputs += [arr for arr, _ in operands] + [final_gain]
    outs = pl.pallas_call(
        functools.partial(_layer_kernel, final_norm=final_norm, seq_tiles=seq_tiles,
                          prep_next=prep_next),
        grid=(n_steps,),
        in_specs=in_specs,
        out_specs=out_specs,
        out_shape=out_shape,
        scratch_shapes=[
            pltpu.VMEM((tile, d), jnp.bfloat16),
            pltpu.VMEM((tile, d), jnp.bfloat16),
            pltpu.VMEM((tile, d), jnp.bfloat16),
            pltpu.VMEM((P_SLOTS, tile, BLOCK_COLS), jnp.float32),
            pltpu.VMEM((2, tile, LANES), jnp.float32),
            pltpu.VMEM((2, tile, LANES), jnp.bfloat16),
            pltpu.VMEM((2, tile, LANES), jnp.bfloat16),
            pltpu.VMEM((2, tile, 2 * LANES), jnp.float32),
            pltpu.VMEM((2, SGU_CHUNK, tile), jnp.float32),
            perm_staging, perm_staging, perm_staging, perm_staging,
            pltpu.VMEM((CONV_K - 1, SUBLANES, d), jnp.float32),
            pltpu.VMEM((LRU_CONV_K - 1, SUBLANES, d), jnp.float32),
            pltpu.VMEM((1, d), jnp.float32),
        ],
        compiler_params=pltpu.CompilerParams(
            dimension_semantics=("arbitrary",),
            vmem_limit_bytes=VMEM_LIMIT_BYTES),
        name="hybrid_mixer_layer",
    )(*inputs)
    return (outs[0], outs[1]) if prep_next else (outs[0], None)


def _block_columns_kernel(*refs):
    o_ref = refs[-1]
    for k in range(N_SLICES):
        o_ref[0, :, k * LANES:(k + 1) * LANES] = refs[k][0].astype(jnp.bfloat16)


def _slice_block_index(j, *, layer, k):
    return (layer, 0, k * N_COL_BLOCKS + j)


def _block_columns(w_in, layer):
    _, d, _ = w_in.shape
    return pl.pallas_call(
        _block_columns_kernel,
        grid=(N_COL_BLOCKS,),
        in_specs=[pl.BlockSpec((1, d, LANES), functools.partial(_slice_block_index, layer=layer, k=k))
                  for k in range(N_SLICES)],
        out_specs=pl.BlockSpec((1, d, BLOCK_COLS), lambda j: (j, 0, 0)),
        out_shape=jax.ShapeDtypeStruct((N_COL_BLOCKS, d, BLOCK_COLS), jnp.bfloat16),
        name="block_columns_bf16",
    )(*([w_in] * N_SLICES))


def _pair_block_diag(w_a, w_x):
    def pairs(w):
        w = w.reshape(w.shape[0], LRU_HEADS // 2, 2, LRU_HEAD_DIM, LRU_HEAD_DIM)
        zero = jnp.zeros_like(w[:, :, 0])
        top = jnp.concatenate([w[:, :, 0], zero], axis=-1)
        bottom = jnp.concatenate([zero, w[:, :, 1]], axis=-1)
        return jnp.concatenate([top, bottom], axis=-2)
    return jnp.concatenate([pairs(w_a), pairs(w_x)], axis=-1)


def kernel(x, c, norm_gain, w_mod, b_mod, w_in, w_out, conv_a_w, sgu_w, sgu_b, lru_conv_w,
           lru_conv_b, lru_wa, lru_ba, lru_wx, lru_bx, lru_lambda, final_gain):
    depth = w_in.shape[0]
    d = x.shape[-1]
    stacked = (
        _modulation(c, w_mod, b_mod),
        norm_gain.reshape(depth, 1, d),
        w_out.astype(jnp.bfloat16),
        conv_a_w,
        sgu_w,
        jnp.broadcast_to(sgu_b[..., None], sgu_b.shape + (LANES,)),
        lru_conv_w,
        lru_conv_b.reshape(depth, 1, d),
        _pair_block_diag(lru_wa, lru_wx).astype(jnp.bfloat16),
        lru_ba.reshape(depth, 1, d),
        lru_bx.reshape(depth, 1, d),
        lru_lambda.reshape(depth, 1, d),
    )
    w_in_blocked = _block_columns(w_in, 0)
    for l in range(depth):
        x, w_in_blocked = _layer(x, l, stacked, w_in_blocked, w_in, final_gain.reshape(1, d),
                                 final_norm=(l == depth - 1))
    return x
```

```python
import functools
import math

import jax
import jax.numpy as jnp
from jax import lax
from jax.experimental import pallas as pl
from jax.experimental.pallas import tpu as pltpu

D_MODEL = 1024
SGU_CHUNK = 128
SGU_HEADS = 8
LRU_HEADS = 16
LRU_HEAD_DIM = 64
CONV_K = 3
LRU_CONV_K = 4
LRU_C = 8.0
EPS = 1e-6
LOG2_E = math.log2(math.e)

SUBLANES = 8
LANES = 128
N_SLICES = 12
(A_X, A_B, A_C, A_Z, S_U, S_V, S_Z, R_X, R_Z, G_A, G_S, G_R) = range(N_SLICES)
N_COL_BLOCKS = D_MODEL // LANES
BLOCK_COLS = N_SLICES * LANES
SEQ_TILE = 512
P_SLOTS = 2
ROW_CHUNK = SUBLANES * SUBLANES
PERM_PITCH = SUBLANES + 1
PERM_ROWS = SUBLANES * PERM_PITCH
VMEM_LIMIT_BYTES = 58 * 1024 * 1024


def _sigmoid(v):
    return 1.0 / (1.0 + jnp.exp2(v * (-LOG2_E)))


def _silu(v):
    return v * _sigmoid(v)


def _mod_kernel(c_ref, w_ref, b_ref, o_ref):
    c = c_ref[...]
    o_ref[0, 0] = jnp.dot(_silu(c), w_ref[0], preferred_element_type=jnp.float32) + b_ref[0, 0]


def _modulation(c, w_mod, b_mod):
    depth, d, _ = w_mod.shape
    bsz = c.shape[0]
    return pl.pallas_call(
        _mod_kernel,
        grid=(depth, 3),
        in_specs=[
            pl.BlockSpec((bsz, d), lambda l, k: (0, 0)),
            pl.BlockSpec((1, d, d), lambda l, k: (l, 0, k)),
            pl.BlockSpec((1, 1, 1, d), lambda l, k: (l, k, 0, 0)),
        ],
        out_specs=pl.BlockSpec((1, 1, bsz, d), lambda l, k: (l, k, 0, 0)),
        out_shape=jax.ShapeDtypeStruct((depth, 3, bsz, d), jnp.float32),
        name="adaln_modulation",
    )(c, w_mod, b_mod.reshape(depth, 3, 1, d))


def _permute_rows(scr, chunk, v3, dyn_zero):
    base = chunk * PERM_ROWS
    for g in range(SUBLANES):
        scr[pl.ds(base + g, SUBLANES, stride=PERM_PITCH), :] = v3[g]
    return jnp.stack([scr[pl.ds(dyn_zero + (base + PERM_PITCH * r), SUBLANES), :]
                      for r in range(SUBLANES)], axis=0)


def _permute_chunk(scr, chunk, xs, dyn_zero):
    cols = []
    for l in range(xs.shape[1] // LANES):
        v3 = xs[:, l * LANES:(l + 1) * LANES].reshape(SUBLANES, SUBLANES, LANES)
        plane = scr.at[l % scr.shape[0]]
        cols.append(_permute_rows(plane, chunk, v3, dyn_zero).reshape(ROW_CHUNK, LANES))
    return jnp.concatenate(cols, axis=1)


def _causal_conv_permuted(vp3, prev_tail, w_ref, col):
    k_taps = w_ref.shape[0]
    n_prev = k_taps - 1
    tail = vp3[SUBLANES - n_prev:]
    sublane = lax.broadcasted_iota(jnp.int32, tail.shape, 1)
    before = jnp.where(sublane >= 1, pltpu.roll(tail, 1, axis=1), pltpu.roll(prev_tail, 1, axis=1))
    ext = jnp.concatenate([before, vp3], axis=0)
    acc = w_ref[k_taps - 1:k_taps, col:col + LANES] * vp3
    for j in range(n_prev):
        acc = acc + w_ref[j:j + 1, col:col + LANES] * ext[j:j + SUBLANES]
    return acc, tail


def _linear_scan_permuted(ap3, bp3, h):
    a_cum = [ap3[0]]
    b_cum = [bp3[0]]
    for r in range(1, SUBLANES):
        b_cum.append(ap3[r] * b_cum[-1] + bp3[r])
        a_cum.append(ap3[r] * a_cum[-1])
    alpha, beta = a_cum[-1], b_cum[-1]
    sublane = lax.broadcasted_iota(jnp.int32, alpha.shape, 0)
    for k in (1, 2, 4):
        keep = sublane >= k
        alpha_prev = jnp.where(keep, pltpu.roll(alpha, k, axis=0), 1.0)
        beta_prev = jnp.where(keep, pltpu.roll(beta, k, axis=0), 0.0)
        beta = alpha * beta_prev + beta
        alpha = alpha * alpha_prev
    after = alpha * h + beta
    entering = jnp.where(sublane >= 1, pltpu.roll(after, 1, axis=0), h)
    states = [a_cum[r] * entering + b_cum[r] for r in range(SUBLANES)]
    return jnp.stack(states, axis=0), after[SUBLANES - 1:SUBLANES, :]


def _layer_kernel(*refs, final_norm, seq_tiles, prep_next, permute_in, unpermute_out):
    refs = list(refs)
    w_raw_ref = refs.pop(2) if prep_next else None
    w_next_ref = refs.pop(17) if prep_next else None
    (x_ref, x_next_ref, mod_ref, gain_ref, w_in_ref, w_out_ref, conv_a_ref, sgu_w_ref,
     sgu_b_ref, lru_cw_ref, lru_cb_ref, lru_w_ref, lru_ba_ref, lru_bx_ref,
     lam_ref, fgain_ref, o_ref,
     h_scr, h_next_scr, m_scr, p_scr, xc_scr, xcb_scr, vn_scr, pre_scr, z_scr,
     perm_scr, cx_carry, rx_carry, h_carry) = refs
    (mod_ref, gain_ref, w_in_ref, w_out_ref, conv_a_ref, sgu_w_ref, sgu_b_ref, lru_cw_ref,
     lru_cb_ref, lru_w_ref, lru_ba_ref, lru_bx_ref, lam_ref) = (
        r.at[0] for r in (mod_ref, gain_ref, w_in_ref, w_out_ref, conv_a_ref, sgu_w_ref,
                          sgu_b_ref, lru_cw_ref, lru_cb_ref, lru_w_ref, lru_ba_ref, lru_bx_ref,
                          lam_ref))
    tile = x_ref.shape[1]
    rows = ROW_CHUNK
    n_row_chunks = tile // rows
    n_sgu_chunks = tile // SGU_CHUNK
    step = pl.program_id(0)
    dyn_zero = step // pl.num_programs(0)
    b = step // seq_tiles
    b_next = jnp.minimum(step + 1, pl.num_programs(0) - 1) // seq_tiles

    @pl.when(step % seq_tiles == 0)
    def _():
        cx_carry[...] = jnp.zeros_like(cx_carry)
        rx_carry[...] = jnp.zeros_like(rx_carry)
        h_carry[...] = jnp.zeros_like(h_carry)

    if prep_next:
        for jb in range(N_COL_BLOCKS):
            for k in range(N_SLICES):
                c0 = k * D_MODEL + jb * LANES
                w_next_ref[jb, :, k * LANES:(k + 1) * LANES] = (
                    w_raw_ref[0, :, c0:c0 + LANES].astype(jnp.bfloat16))

    def residual_chunk(src_ref, c):
        xs = src_ref[0, c * rows:(c + 1) * rows, :]
        return _permute_chunk(perm_scr, c, xs, dyn_zero) if permute_in else xs

    def modulated_norm(src_ref, batch, dst_ref):
        shift = mod_ref[0, pl.ds(batch, 1), :]
        scale = mod_ref[1, pl.ds(batch, 1), :]
        gain_scale = gain_ref[...] * (1.0 + scale)
        for c in range(n_row_chunks):
            r0 = c * rows
            xs = residual_chunk(src_ref, c)
            inv = lax.rsqrt(jnp.mean(xs * xs, axis=-1, keepdims=True) + EPS)
            dst_ref[r0:r0 + rows, :] = ((xs * inv) * gain_scale + shift).astype(jnp.bfloat16)

    def project(j, operand_ref, half):
        cols = slice(half * (BLOCK_COLS // 2), (half + 1) * (BLOCK_COLS // 2))
        p_scr[j % P_SLOTS, :, cols] = jnp.dot(operand_ref[...], w_in_ref[j, :, cols],
                                              preferred_element_type=jnp.float32)

    @pl.when(step == 0)
    def _():
        modulated_norm(x_ref, b, h_scr)
        project(0, h_scr, 0)
        project(0, h_scr, 1)

    modulated_norm(x_next_ref, b_next, h_next_scr)
    gate = mod_ref[2, pl.ds(b, 1), :]

    def position_of(row):
        return (row & -ROW_CHUNK) | ((row & (SUBLANES - 1)) * SUBLANES) | (
            (row // SUBLANES) & (SUBLANES - 1))

    causal = (position_of(lax.broadcasted_iota(jnp.int32, (SGU_CHUNK, SGU_CHUNK), 0))
              >= position_of(lax.broadcasted_iota(jnp.int32, (SGU_CHUNK, SGU_CHUNK), 1)))

    def mix(j):
        col = j * LANES
        lanes = slice(col, col + LANES)
        p = p_scr.at[j % P_SLOTS]
        xc_s, xcb_s, vn_s, pre_s, z_s = (
            r.at[j % 2] for r in (xc_scr, xcb_scr, vn_scr, pre_scr, z_scr))

        def sl(k, r0):
            return p[r0:r0 + rows, k * LANES:(k + 1) * LANES]

        def sl3(k, r0):
            return sl(k, r0).reshape(SUBLANES, SUBLANES, LANES)

        rx_tail = rx_carry[:, :, lanes]
        for c in range(n_row_chunks):
            r0 = c * rows
            xcp, rx_tail = _causal_conv_permuted(sl3(R_X, r0), rx_tail, lru_cw_ref, col)
            xc = (xcp + lru_cb_ref[:, lanes]).reshape(rows, LANES)
            xc_s[r0:r0 + rows, :] = xc
            xcb_s[r0:r0 + rows, :] = xc.astype(jnp.bfloat16)
            v = sl(S_V, r0)
            vc = v - jnp.mean(v, axis=-1, keepdims=True)
            var = jnp.mean(vc * vc, axis=-1, keepdims=True)
            vn_s[r0:r0 + rows, :] = (vc * lax.rsqrt(var + EPS)).astype(jnp.bfloat16)
        rx_carry[:, :, lanes] = rx_tail

        pre_s[...] = jnp.dot(xcb_s[...], lru_w_ref[j], preferred_element_type=jnp.float32)
        vn_wide = jnp.concatenate(
            [vn_s[n * SGU_CHUNK:(n + 1) * SGU_CHUNK, :] for n in range(n_sgu_chunks)], axis=1)
        w_s = jnp.where(causal, sgu_w_ref[j], 0.0).astype(jnp.bfloat16)
        z_s[...] = jnp.dot(w_s, vn_wide, preferred_element_type=jnp.float32)
        yield

        lam = lam_ref[:, lanes]
        softplus_neg_lam = jnp.maximum(-lam, 0.0) + jnp.log1p(jnp.exp(-jnp.abs(lam)))
        log2_a_per_r = (-LRU_C * LOG2_E) * softplus_neg_lam
        cx_tail = cx_carry[:, :, lanes]
        state = h_carry[:, lanes]
        for c in range(n_row_chunks):
            r0 = c * rows
            chunk, q0 = divmod(r0, SGU_CHUNK)

            convp, cx_tail = _causal_conv_permuted(sl3(A_C, r0) * sl3(A_X, r0), cx_tail,
                                                   conv_a_ref, col)
            conv = convp.reshape(rows, LANES)
            merged = _sigmoid(sl(G_A, r0)) * ((_silu(sl(A_Z, r0)) * sl(A_B, r0)) * conv)

            z = (z_s[q0:q0 + rows, chunk * LANES:(chunk + 1) * LANES]
                 + sgu_b_ref[j, q0:q0 + rows, :])
            merged = merged + _sigmoid(sl(G_S, r0)) * ((_silu(sl(S_Z, r0)) * sl(S_U, r0)) * z)

            r =_sigmoid(pre_s[r0:r0 + rows, :LANES] + lru_ba_ref[:, lanes])
            i = _sigmoid(pre_s[r0:r0 + rows, LANES:] + lru_bx_ref[:, lanes])
            a = jnp.exp2(r * log2_a_per_r)
            y = 1.0 - a * a
            bb = (y * lax.rsqrt(jnp.maximum(y, 1e-30))) * (i * xc_s[r0:r0 + rows, :])
            hsp, state = _linear_scan_permuted(a.reshape(SUBLANES, SUBLANES, LANES),
                                               bb.reshape(SUBLANES, SUBLANES, LANES), state)
            merged = merged + _sigmoid(sl(G_R, r0)) * (
                _silu(sl(R_Z, r0)) * hsp.reshape(rows, LANES))

            m_scr[r0:r0 + rows, lanes] = merged.astype(jnp.bfloat16)
        cx_carry[:, :, lanes] = cx_tail
        h_carry[:, lanes] = state

    for j in range(N_COL_BLOCKS):
        upcoming = (j + 1, h_scr) if j + 1 < N_COL_BLOCKS else (0, h_next_scr)
        mix_parts = mix(j)
        project(*upcoming, 0)
        next(mix_parts)
        project(*upcoming, 1)
        next(mix_parts, None)

    res = p_scr.at[(N_COL_BLOCKS - 1) % P_SLOTS]
    res[:, :D_MODEL] = jnp.dot(m_scr[...], w_out_ref[...], preferred_element_type=jnp.float32)
    for c in range(n_row_chunks):
        r0 = c * rows
        out = residual_chunk(x_ref, c) + gate * res[r0:r0 + rows, :D_MODEL]
        if final_norm:
            inv_o = lax.rsqrt(jnp.mean(out * out, axis=-1, keepdims=True) + EPS)
            out = (out * inv_o) * fgain_ref[...]
        if unpermute_out:
            out = _permute_chunk(perm_scr, c, out, dyn_zero)
        o_ref[0, r0:r0 + rows, :] = out
    h_scr[...] = h_next_scr[...]


def _layer_resident(arr, layer):
    index = (layer,) + (0,) * (arr.ndim - 1)
    return pl.BlockSpec((1,) + arr.shape[1:], lambda i: index, pipeline_mode=pl.Buffered(1))


def _tile_index(i, *, seq_tiles, last, ahead):
    t = jnp.minimum(i + ahead, last)
    return (t // seq_tiles, t % seq_tiles, 0)


def _layer(x, layer, stacked, w_in_blocked, w_in_raw, final_gain, *, last):
    bsz, seq, d = x.shape
    tile = SEQ_TILE
    seq_tiles = seq // tile
    n_steps = bsz * seq_tiles
    prep_next = layer + 1 < w_in_raw.shape[0]
    slab = d // n_steps
    tile_spec = functools.partial(_tile_index, seq_tiles=seq_tiles, last=n_steps - 1)
    operands =[(arr, layer) for arr in stacked]
    operands.insert(2, (w_in_blocked[None], 0))
    in_specs = [pl.BlockSpec((1, tile, d), functools.partial(tile_spec, ahead=0)),
                pl.BlockSpec((1, tile, d), functools.partial(tile_spec, ahead=1))]
    inputs = [x, x]
    out_specs = [pl.BlockSpec((1, tile, d), functools.partial(tile_spec, ahead=0))]
    out_shape = [jax.ShapeDtypeStruct(x.shape, x.dtype)]
    if prep_next:
        in_specs.append(pl.BlockSpec((1, slab, w_in_raw.shape[2]), lambda i: (layer + 1, i, 0)))
        inputs.append(w_in_raw)
        out_specs.append(pl.BlockSpec((N_COL_BLOCKS, slab, BLOCK_COLS), lambda i: (0, i, 0)))
        out_shape.append(jax.ShapeDtypeStruct((N_COL_BLOCKS, d, BLOCK_COLS), jnp.bfloat16))
    in_specs += [_layer_resident(arr, index) for arr, index in operands]
    in_specs.append(pl.BlockSpec(final_gain.shape, lambda i: (0, 0), pipeline_mode=pl.Buffered(1)))
    inputs += [arr for arr, _ in operands] + [final_gain]
    outs = pl.pallas_call(
        functools.partial(_layer_kernel, final_norm=last, seq_tiles=seq_tiles,
                          prep_next=prep_next, permute_in=(layer == 0), unpermute_out=last),
        grid=(n_steps,),
        in_specs=in_specs,
        out_specs=out_specs,
        out_shape=out_shape,
        scratch_shapes=[
            pltpu.VMEM((tile, d), jnp.bfloat16),
            pltpu.VMEM((tile, d), jnp.bfloat16),
            pltpu.VMEM((tile, d), jnp.bfloat16),
            pltpu.VMEM((P_SLOTS, tile, BLOCK_COLS), jnp.float32),
            pltpu.VMEM((2, tile, LANES), jnp.float32),
            pltpu.VMEM((2, tile, LANES), jnp.bfloat16),
            pltpu.VMEM((2, tile, LANES), jnp.bfloat16),
            pltpu.VMEM((2, tile, 2 * LANES), jnp.float32),
            pltpu.VMEM((2, SGU_CHUNK, tile), jnp.float32),
            pltpu.VMEM((2, tile // ROW_CHUNK * PERM_ROWS, LANES), jnp.float32),
            pltpu.VMEM((CONV_K - 1, SUBLANES, d), jnp.float32),
            pltpu.VMEM((LRU_CONV_K - 1, SUBLANES, d), jnp.float32),
            pltpu.VMEM((1, d), jnp.float32),
        ],
        compiler_params=pltpu.CompilerParams(
            dimension_semantics=("arbitrary",),
            vmem_limit_bytes=VMEM_LIMIT_BYTES),
        name="hybrid_mixer_layer",
    )(*inputs)
    return (outs[0], outs[1]) if prep_next else (outs[0], None)


def _block_columns_kernel(*refs):
    o_ref = refs[-1]
    for k in range(N_SLICES):
        o_ref[0, :, k * LANES:(k + 1) * LANES] = refs[k][0].astype(jnp.bfloat16)


def _slice_block_index(j, *, layer, k):
    return (layer, 0, k * N_COL_BLOCKS + j)


def _block_columns(w_in, layer):
    _, d, _ = w_in.shape
    return pl.pallas_call(
        _block_columns_kernel,
        grid=(N_COL_BLOCKS,),
        in_specs=[pl.BlockSpec((1, d, LANES), functools.partial(_slice_block_index, layer=layer, k=k))
                  for k in range(N_SLICES)],
        out_specs=pl.BlockSpec((1, d, BLOCK_COLS), lambda j: (j, 0, 0)),
        out_shape=jax.ShapeDtypeStruct((N_COL_BLOCKS, d, BLOCK_COLS), jnp.bfloat16),
        name="block_columns_bf16",
    )(*([w_in] * N_SLICES))


def _permute_positions(arr, axis):
    axis = axis % arr.ndim
    shape = arr.shape
    split = arr.reshape(shape[:axis] + (SGU_CHUNK // ROW_CHUNK, SUBLANES, SUBLANES) + shape[axis + 1:])
    return jnp.swapaxes(split, axis + 1, axis + 2).reshape(shape)


def _pair_block_diag(w_a, w_x):
    def pairs(w):
        w = w.reshape(w.shape[0], LRU_HEADS // 2, 2, LRU_HEAD_DIM, LRU_HEAD_DIM)
        zero = jnp.zeros_like(w[:, :, 0])
        top = jnp.concatenate([w[:, :, 0], zero], axis=-1)
        bottom = jnp.concatenate([zero, w[:, :, 1]], axis=-1)
        return jnp.concatenate([top, bottom], axis=-2)
    return jnp.concatenate([pairs(w_a), pairs(w_x)], axis=-1)


def kernel(x, c, norm_gain, w_mod, b_mod, w_in, w_out, conv_a_w, sgu_w, sgu_b, lru_conv_w,
           lru_conv_b, lru_wa, lru_ba, lru_wx, lru_bx, lru_lambda, final_gain):
    depth = w_in.shape[0]
    d = x.shape[-1]
    stacked = (
        _modulation(c, w_mod, b_mod),
        norm_gain.reshape(depth, 1, d),
        w_out.astype(jnp.bfloat16),
        conv_a_w,
        _permute_positions(_permute_positions(sgu_w, -1), -2),
        jnp.broadcast_to(_permute_positions(sgu_b, -1)[..., None], sgu_b.shape + (LANES,)),
        lru_conv_w,
        lru_conv_b.reshape(depth, 1, d),
        _pair_block_diag(lru_wa, lru_wx).astype(jnp.bfloat16),
        lru_ba.reshape(depth, 1, d),
        lru_bx.reshape(depth, 1, d),
        lru_lambda.reshape(depth, 1, d),
    )
    w_in_blocked = _block_columns(w_in, 0)
    for l in range(depth):
        x, w_in_blocked = _layer(x, l, stacked, w_in_blocked, w_in, final_gain.reshape(1, d),
                                 last=(l == depth - 1))
    return x
```

```python
import functools
import math

import jax
import jax.numpy as jnp
from jax import lax
from jax.experimental import pallas as pl
from jax.experimental.pallas import tpu as pltpu

D_MODEL = 1024
SGU_CHUNK = 128
SGU_HEADS = 8
LRU_HEADS = 16
LRU_HEAD_DIM = 64
CONV_K = 3
LRU_CONV_K = 4
LRU_C = 8.0
EPS = 1e-6
LOG2_E = math.log2(math.e)

SUBLANES = 8
LANES = 128
N_SLICES = 12
(A_X, A_B, A_C, A_Z, S_U, S_V, S_Z, R_X, R_Z, G_A, G_S, G_R) = range(N_SLICES)
N_COL_BLOCKS = D_MODEL // LANES
BLOCK_COLS = N_SLICES * LANES
SEQ_TILE = 512
P_SLOTS = 2
ROW_CHUNK = SUBLANES * SUBLANES
PERM_PITCH = SUBLANES + 1
PERM_ROWS = SUBLANES * PERM_PITCH
VMEM_LIMIT_BYTES = 58 * 1024 * 1024


def _sigmoid(v):
    return 1.0 / (1.0 + jnp.exp2(v * (-LOG2_E)))


def _silu(v):
    return v * _sigmoid(v)


def _mod_kernel(c_ref, w_ref, b_ref, o_ref):
    c = c_ref[...]
    o_ref[0, 0] = jnp.dot(_silu(c), w_ref[0], preferred_element_type=jnp.float32) + b_ref[0, 0]


def _modulation(c, w_mod, b_mod):
    depth, d, _ = w_mod.shape
    bsz = c.shape[0]
    return pl.pallas_call(
        _mod_kernel,
        grid=(depth, 3),
        in_specs=[
            pl.BlockSpec((bsz, d), lambda l, k: (0, 0)),
            pl.BlockSpec((1, d, d), lambda l, k: (l, 0, k)),
            pl.BlockSpec((1, 1, 1, d), lambda l, k: (l, k, 0, 0)),
        ],
        out_specs=pl.BlockSpec((1, 1, bsz, d), lambda l, k: (l, k, 0, 0)),
        out_shape=jax.ShapeDtypeStruct((depth, 3, bsz, d), jnp.float32),
        name="adaln_modulation",
    )(c, w_mod, b_mod.reshape(depth, 3, 1, d))


def _permute_rows(scr, chunk, v3, dyn_zero):
    base = chunk * PERM_ROWS
    for g in range(SUBLANES):
        scr[pl.ds(base + g, SUBLANES, stride=PERM_PITCH), :] = v3[g]
    return jnp.stack([scr[pl.ds(dyn_zero + (base + PERM_PITCH * r), SUBLANES), :]
                      for r in range(SUBLANES)], axis=0)


def _permute_chunk(scr, chunk, xs, dyn_zero):
    cols = []
    for l in range(xs.shape[1] // LANES):
        v3 = xs[:, l * LANES:(l + 1) * LANES].reshape(SUBLANES, SUBLANES, LANES)
        plane = scr.at[l % scr.shape[0]]
        cols.append(_permute_rows(plane, chunk, v3, dyn_zero).reshape(ROW_CHUNK, LANES))
    return jnp.concatenate(cols, axis=1)


def _causal_conv_permuted(vp3, prev_tail, w_ref, col):
    k_taps = w_ref.shape[0]
    n_prev = k_taps - 1
    tail = vp3[SUBLANES - n_prev:]
    sublane = lax.broadcasted_iota(jnp.int32, tail.shape, 1)
    before = jnp.where(sublane >= 1, pltpu.roll(tail, 1, axis=1), pltpu.roll(prev_tail, 1, axis=1))
    ext = jnp.concatenate([before, vp3], axis=0)
    acc = w_ref[k_taps - 1:k_taps, col:col + LANES] * vp3
    for j in range(n_prev):
        acc = acc + w_ref[j:j + 1, col:col + LANES] * ext[j:j + SUBLANES]
    return acc, tail


def _linear_scan_permuted(ap3, bp3, h):
    a_cum = [ap3[0]]
    b_cum = [bp3[0]]
    for r in range(1, SUBLANES):
        b_cum.append(ap3[r] * b_cum[-1] + bp3[r])
        a_cum.append(ap3[r] * a_cum[-1])
    alpha, beta = a_cum[-1], b_cum[-1]
    sublane = lax.broadcasted_iota(jnp.int32, alpha.shape, 0)
    for k in (1, 2, 4):
        keep = sublane >= k
        alpha_prev = jnp.where(keep, pltpu.roll(alpha, k, axis=0), 1.0)
        beta_prev = jnp.where(keep, pltpu.roll(beta, k, axis=0), 0.0)
        beta = alpha * beta_prev + beta
        alpha = alpha * alpha_prev
    after = alpha * h + beta
    entering = jnp.where(sublane >= 1, pltpu.roll(after, 1, axis=0), h)
    states = [a_cum[r] * entering + b_cum[r] for r in range(SUBLANES)]
    return jnp.stack(states, axis=0), after[SUBLANES - 1:SUBLANES, :]


def _layer_kernel(*refs, final_norm, seq_tiles, prep_next, permute_in, unpermute_out):
    refs = list(refs)
    w_raw_ref = refs.pop(2) if prep_next else None
    w_next_ref = refs.pop(17) if prep_next else None
    (x_ref, x_next_ref, mod_ref, gain_ref, w_in_ref, w_out_ref, conv_a_ref, sgu_w_ref,
     sgu_b_ref, lru_cw_ref, lru_cb_ref, lru_w_ref, lru_ba_ref, lru_bx_ref,
     lam_ref, fgain_ref, o_ref,
     h_scr, h_next_scr, m_scr, p_scr, xc_scr, xcb_scr, vn_scr, pre_scr, z_scr, sgu_wp_scr,
     perm_scr, cx_carry, rx_carry, h_carry) = refs
    (mod_ref, gain_ref, w_in_ref, w_out_ref, conv_a_ref, sgu_w_ref, sgu_b_ref, lru_cw_ref,
     lru_cb_ref, lru_w_ref, lru_ba_ref, lru_bx_ref, lam_ref) = (
        r.at[0] for r in (mod_ref, gain_ref, w_in_ref, w_out_ref, conv_a_ref, sgu_w_ref,
                          sgu_b_ref, lru_cw_ref, lru_cb_ref, lru_w_ref, lru_ba_ref, lru_bx_ref,
                          lam_ref))
    tile = x_ref.shape[1]
    rows = ROW_CHUNK
    n_row_chunks = tile // rows
    n_sgu_chunks = tile // SGU_CHUNK
    step = pl.program_id(0)
    dyn_zero = step // pl.num_programs(0)
    b = step // seq_tiles
    b_next = jnp.minimum(step + 1, pl.num_programs(0) - 1) // seq_tiles

    @pl.when(step % seq_tiles == 0)
    def _():
        cx_carry[...] = jnp.zeros_like(cx_carry)
        rx_carry[...] = jnp.zeros_like(rx_carry)
        h_carry[...] = jnp.zeros_like(h_carry)

    if prep_next:
        for jb in range(N_COL_BLOCKS):
            for k in range(N_SLICES):
                c0 = k * D_MODEL + jb * LANES
                w_next_ref[jb, :, k * LANES:(k + 1) * LANES] = (
                    w_raw_ref[0, :, c0:c0 + LANES].astype(jnp.bfloat16))

    def residual_chunk(src_ref, c):
        xs = src_ref[0, c * rows:(c + 1) * rows, :]
        return _permute_chunk(perm_scr, c, xs, dyn_zero) if permute_in else xs

    def modulated_norm(src_ref, batch, dst_ref):
        shift = mod_ref[0, pl.ds(batch, 1), :]
        scale = mod_ref[1, pl.ds(batch, 1), :]
        gain_scale = gain_ref[...] * (1.0 + scale)
        for c in range(n_row_chunks):
            r0 = c * rows
            xs = residual_chunk(src_ref, c)
            inv = lax.rsqrt(jnp.mean(xs * xs, axis=-1, keepdims=True) + EPS)
            dst_ref[r0:r0 + rows, :] = ((xs * inv) * gain_scale + shift).astype(jnp.bfloat16)

    def project(j, operand_ref, half):
        cols = slice(half * (BLOCK_COLS // 2), (half + 1) * (BLOCK_COLS // 2))
        p_scr[j % P_SLOTS, :, cols] = jnp.dot(operand_ref[...], w_in_ref[j, :, cols],
                                              preferred_element_type=jnp.float32)

    @pl.when(step == 0)
    def _():
        modulated_norm(x_ref, b, h_scr)
        project(0, h_scr, 0)
        project(0, h_scr, 1)

    modulated_norm(x_next_ref, b_next, h_next_scr)
    gate = mod_ref[2, pl.ds(b, 1), :]

    def position_of(row):
        return (row & -ROW_CHUNK) | ((row & (SUBLANES - 1)) * SUBLANES) | (
            (row // SUBLANES) & (SUBLANES - 1))

    @pl.when(step == 0)
    def _():
        row = lax.broadcasted_iota(jnp.int32, (SGU_CHUNK, SGU_CHUNK), 0)
        col = lax.broadcasted_iota(jnp.int32, (SGU_CHUNK, SGU_CHUNK), 1)
        select_rows = (col == position_of(row)).astype(jnp.bfloat16)
        select_cols = (row == position_of(col)).astype(jnp.bfloat16)
        for head in range(SGU_HEADS):
            w = jnp.where(row >= col, sgu_w_ref[head], 0.0).astype(jnp.bfloat16)
            w = jnp.dot(w, select_cols, preferred_element_type=jnp.float32).astype(jnp.bfloat16)
            sgu_wp_scr[head] = jnp.dot(select_rows, w, preferred_element_type=jnp.float32
                                       ).astype(jnp.bfloat16)

    def mix(j):
        col = j * LANES
        lanes = slice(col, col + LANES)
        p = p_scr.at[j % P_SLOTS]
        xc_s = xc_scr.at[j % 2]

        def sl(k, r0):
            return p[r0:r0 + rows, k * LANES:(k + 1) * LANES]

        def sl3(k, r0):
            return sl(k, r0).reshape(SUBLANES, SUBLANES, LANES)

        rx_tail = rx_carry[:, :, lanes]
        for c in range(n_row_chunks):
            r0 = c * rows
            xcp, rx_tail = _causal_conv_permuted(sl3(R_X, r0), rx_tail, lru_cw_ref, col)
            xc = (xcp + lru_cb_ref[:, lanes]).reshape(rows, LANES)
            xc_s[r0:r0 + rows, :] = xc
            xcb_scr[r0:r0 + rows, :] = xc.astype(jnp.bfloat16)
            v = sl(S_V, r0)
            vc = v - jnp.mean(v, axis=-1, keepdims=True)
            var = jnp.mean(vc * vc, axis=-1, keepdims=True)
            vn_scr[r0:r0 + rows, :] = (vc * lax.rsqrt(var + EPS)).astype(jnp.bfloat16)
        rx_carry[:, :, lanes] = rx_tail

        pre_scr[...] = jnp.dot(xcb_scr[...], lru_w_ref[j], preferred_element_type=jnp.float32)
        vn_wide = jnp.concatenate(
            [vn_scr[n * SGU_CHUNK:(n + 1) * SGU_CHUNK, :] for n in range(n_sgu_chunks)], axis=1)
        z_scr[...] = jnp.dot(sgu_wp_scr[j], vn_wide, preferred_element_type=jnp.float32)
        yield

        lam = lam_ref[:, lanes]
        softplus_neg_lam = jnp.maximum(-lam, 0.0) + jnp.log1p(jnp.exp(-jnp.abs(lam)))
        log2_a_per_r = (-LRU_C * LOG2_E) * softplus_neg_lam
        cx_tail = cx_carry[:, :, lanes]
        state = h_carry[:, lanes]
        for c in range(n_row_chunks):
            r0 = c * rows
            chunk, q0 = divmod(r0, SGU_CHUNK)

            convp, cx_tail = _causal_conv_permuted(sl3(A_C, r0) * sl3(A_X, r0), cx_tail,
                                                   conv_a_ref, col)
            conv = convp.reshape(rows, LANES)
            merged = _sigmoid(sl(G_A, r0)) * ((_silu(sl(A_Z, r0)) * sl(A_B, r0)) * conv)

            z = (z_scr[q0:q0 + rows, chunk * LANES:(chunk + 1) * LANES]
                 + sgu_b_ref[j, q0:q0 + rows, :])
            merged = merged + _sigmoid(sl(G_S, r0)) * ((_silu(sl(S_Z, r0)) * sl(S_U, r0)) * z)

            r =_sigmoid(pre_scr[r0:r0 + rows, :LANES] + lru_ba_ref[:, lanes])
            i = _sigmoid(pre_scr[r0:r0 + rows, LANES:] + lru_bx_ref[:, lanes])
            a = jnp.exp2(r * log2_a_per_r)
            y = 1.0 - a * a
            bb = (y * lax.rsqrt(jnp.maximum(y, 1e-30))) * (i * xc_s[r0:r0 + rows, :])
            hsp, state = _linear_scan_permuted(a.reshape(SUBLANES, SUBLANES, LANES),
                                               bb.reshape(SUBLANES, SUBLANES, LANES), state)
            merged = merged + _sigmoid(sl(G_R, r0)) * (
                _silu(sl(R_Z, r0)) * hsp.reshape(rows, LANES))

            m_scr[r0:r0 + rows, lanes] = merged.astype(jnp.bfloat16)
        cx_carry[:, :, lanes] = cx_tail
        h_carry[:, lanes] = state

    for j in range(N_COL_BLOCKS):
        upcoming = (j + 1, h_scr) if j + 1 < N_COL_BLOCKS else (0, h_next_scr)
        mix_parts = mix(j)
        project(*upcoming, 0)
        next(mix_parts)
        project(*upcoming, 1)
        next(mix_parts, None)

    res = p_scr.at[(N_COL_BLOCKS - 1) % P_SLOTS]
    res[:, :D_MODEL] = jnp.dot(m_scr[...], w_out_ref[...], preferred_element_type=jnp.float32)
    for c in range(n_row_chunks):
        r0 = c * rows
        out = residual_chunk(x_ref, c) + gate * res[r0:r0 + rows, :D_MODEL]
        if final_norm:
            inv_o = lax.rsqrt(jnp.mean(out * out, axis=-1, keepdims=True) + EPS)
            out = (out * inv_o) * fgain_ref[...]
        if unpermute_out:
            out = _permute_chunk(perm_scr, c, out, dyn_zero)
        o_ref[0, r0:r0 + rows, :] = out
    h_scr[...] = h_next_scr[...]


def _layer_resident(arr, layer):
    index = (layer,) + (0,) * (arr.ndim - 1)
    return pl.BlockSpec((1,) + arr.shape[1:], lambda i: index, pipeline_mode=pl.Buffered(1))


def _tile_index(i, *, seq_tiles, last, ahead):
    t = jnp.minimum(i + ahead, last)
    return (t // seq_tiles, t % seq_tiles, 0)


def _layer(x, layer, stacked, w_in_blocked, w_in_raw, final_gain, *, last):
    bsz, seq, d = x.shape
    tile = SEQ_TILE
    seq_tiles = seq // tile
    n_steps = bsz * seq_tiles
    prep_next = layer + 1 < w_in_raw.shape[0]
    slab = d // n_steps
    tile_spec = functools.partial(_tile_index, seq_tiles=seq_tiles, last=n_steps - 1)
    operands =[(arr, layer) for arr in stacked]
    operands.insert(2, (w_in_blocked[None], 0))
    in_specs = [pl.BlockSpec((1, tile, d), functools.partial(tile_spec, ahead=0)),
                pl.BlockSpec((1, tile, d), functools.partial(tile_spec, ahead=1))]
    inputs = [x, x]
    out_specs = [pl.BlockSpec((1, tile, d), functools.partial(tile_spec, ahead=0))]
    out_shape = [jax.ShapeDtypeStruct(x.shape, x.dtype)]
    if prep_next:
        in_specs.append(pl.BlockSpec((1, slab, w_in_raw.shape[2]), lambda i: (layer + 1, i, 0)))
        inputs.append(w_in_raw)
        out_specs.append(pl.BlockSpec((N_COL_BLOCKS, slab, BLOCK_COLS), lambda i: (0, i, 0)))
        out_shape.append(jax.ShapeDtypeStruct((N_COL_BLOCKS, d, BLOCK_COLS), jnp.bfloat16))
    in_specs += [_layer_resident(arr, index) for arr, index in operands]
    in_specs.append(pl.BlockSpec(final_gain.shape, lambda i: (0, 0), pipeline_mode=pl.Buffered(1)))
    inputs += [arr for arr, _ in operands] + [final_gain]
    outs = pl.pallas_call(
        functools.partial(_layer_kernel, final_norm=last, seq_tiles=seq_tiles,
                          prep_next=prep_next, permute_in=(layer == 0), unpermute_out=last),
        grid=(n_steps,),
        in_specs=in_specs,
        out_specs=out_specs,
        out_shape=out_shape,
        scratch_shapes=[
            pltpu.VMEM((tile, d), jnp.bfloat16),
            pltpu.VMEM((tile, d), jnp.bfloat16),
            pltpu.VMEM((tile, d), jnp.bfloat16),
            pltpu.VMEM((P_SLOTS, tile, BLOCK_COLS), jnp.float32),
            pltpu.VMEM((2, tile, LANES), jnp.float32),
            pltpu.VMEM((tile, LANES), jnp.bfloat16),
            pltpu.VMEM((tile, LANES), jnp.bfloat16),
            pltpu.VMEM((tile, 2 * LANES), jnp.float32),
            pltpu.VMEM((SGU_CHUNK, tile), jnp.float32),
            pltpu.VMEM((SGU_HEADS, SGU_CHUNK, SGU_CHUNK), jnp.bfloat16),
            pltpu.VMEM((2, tile // ROW_CHUNK * PERM_ROWS, LANES), jnp.float32),
            pltpu.VMEM((CONV_K - 1, SUBLANES, d), jnp.float32),
            pltpu.VMEM((LRU_CONV_K - 1, SUBLANES, d), jnp.float32),
            pltpu.VMEM((1, d), jnp.float32),
        ],
        compiler_params=pltpu.CompilerParams(
            dimension_semantics=("arbitrary",),
            vmem_limit_bytes=VMEM_LIMIT_BYTES),
        name="hybrid_mixer_layer",
    )(*inputs)
    return (outs[0], outs[1]) if prep_next else (outs[0], None)


def _block_columns_kernel(*refs):
    o_ref = refs[-1]
    for k in range(N_SLICES):
        o_ref[0, :, k * LANES:(k + 1) * LANES] = refs[k][0].astype(jnp.bfloat16)


def _slice_block_index(j, *, layer, k):
    return (layer, 0, k * N_COL_BLOCKS + j)


def _block_columns(w_in, layer):
    _, d, _ = w_in.shape
    return pl.pallas_call(
        _block_columns_kernel,
        grid=(N_COL_BLOCKS,),
        in_specs=[pl.BlockSpec((1, d, LANES), functools.partial(_slice_block_index, layer=layer, k=k))
                  for k in range(N_SLICES)],
        out_specs=pl.BlockSpec((1, d, BLOCK_COLS), lambda j: (j, 0, 0)),
        out_shape=jax.ShapeDtypeStruct((N_COL_BLOCKS, d, BLOCK_COLS), jnp.bfloat16),
        name="block_columns_bf16",
    )(*([w_in] * N_SLICES))


def _permute_positions(arr, axis):
    axis = axis % arr.ndim
    shape = arr.shape
    split = arr.reshape(shape[:axis] + (SGU_CHUNK // ROW_CHUNK, SUBLANES, SUBLANES) + shape[axis + 1:])
    return jnp.swapaxes(split, axis + 1, axis + 2).reshape(shape)


def _pair_block_diag(w_a, w_x):
    def pairs(w):
        w = w.reshape(w.shape[0], LRU_HEADS // 2, 2, LRU_HEAD_DIM, LRU_HEAD_DIM)
        zero = jnp.zeros_like(w[:, :, 0])
        top = jnp.concatenate([w[:, :, 0], zero], axis=-1)
        bottom = jnp.concatenate([zero, w[:, :, 1]], axis=-1)
        return jnp.concatenate([top, bottom], axis=-2)
    return jnp.concatenate([pairs(w_a), pairs(w_x)], axis=-1)


def kernel(x, c, norm_gain, w_mod, b_mod, w_in, w_out, conv_a_w, sgu_w, sgu_b, lru_conv_w,
           lru_conv_b, lru_wa, lru_ba, lru_wx, lru_bx, lru_lambda, final_gain):
    depth = w_in.shape[0]
    d = x.shape[-1]
    stacked = (
        _modulation(c, w_mod, b_mod),
        norm_gain.reshape(depth, 1, d),
        w_out.astype(jnp.bfloat16),
        conv_a_w,
        sgu_w,
        jnp.broadcast_to(_permute_positions(sgu_b, -1)[..., None], sgu_b.shape + (LANES,)),
        lru_conv_w,
        lru_conv_b.reshape(depth, 1, d),
        _pair_block_diag(lru_wa, lru_wx).astype(jnp.bfloat16),
        lru_ba.reshape(depth, 1, d),
        lru_bx.reshape(depth, 1, d),
        lru_lambda.reshape(depth, 1, d),
    )
    w_in_blocked = _block_columns(w_in, 0)
    for l in range(depth):
        x, w_in_blocked = _layer(x, l, stacked, w_in_blocked, w_in, final_gain.reshape(1, d),
                                 last=(l == depth - 1))
    return x
```

```python
import functools
import math

import jax
import jax.numpy as jnp
from jax import lax
from jax.experimental import pallas as pl
from jax.experimental.pallas import tpu as pltpu

D_MODEL = 1024
SGU_CHUNK = 128
SGU_HEADS = 8
LRU_HEADS = 16
LRU_HEAD_DIM = 64
CONV_K = 3
LRU_CONV_K = 4
LRU_C = 8.0
EPS = 1e-6
LOG2_E = math.log2(math.e)

SUBLANES = 8
LANES = 128
N_SLICES = 12
(A_X, A_B, A_C, A_Z, S_U, S_V, S_Z, R_X, R_Z, G_A, G_S, G_R) = range(N_SLICES)
N_COL_BLOCKS = D_MODEL // LANES
BLOCK_COLS = N_SLICES * LANES
SEQ_TILE = 512
P_SLOTS = 2
ROW_CHUNK = SUBLANES * SUBLANES
PERM_PITCH = SUBLANES + 1
PERM_ROWS = SUBLANES * PERM_PITCH
VMEM_LIMIT_BYTES = 60 * 1024 * 1024


def _sigmoid(v):
    return 1.0 / (1.0 + jnp.exp2(v * (-LOG2_E)))


def _silu(v):
    return v * _sigmoid(v)


def _mod_kernel(c_ref, w_ref, b_ref, o_ref):
    c = c_ref[...]
    o_ref[0, 0] = jnp.dot(_silu(c), w_ref[0], preferred_element_type=jnp.float32) + b_ref[0, 0]


def _modulation(c, w_mod, b_mod):
    depth, d, _ = w_mod.shape
    bsz = c.shape[0]
    return pl.pallas_call(
        _mod_kernel,
        grid=(depth, 3),
        in_specs=[
            pl.BlockSpec((bsz, d), lambda l, k: (0, 0)),
            pl.BlockSpec((1, d, d), lambda l, k: (l, 0, k)),
            pl.BlockSpec((1, 1, 1, d), lambda l, k: (l, k, 0, 0)),
        ],
        out_specs=pl.BlockSpec((1, 1, bsz, d), lambda l, k: (l, k, 0, 0)),
        out_shape=jax.ShapeDtypeStruct((depth, 3, bsz, d), jnp.float32),
        name="adaln_modulation",
    )(c, w_mod, b_mod.reshape(depth, 3, 1, d))


def _permute_rows(scr, chunk, v3, dyn_zero):
    base = chunk * PERM_ROWS
    for g in range(SUBLANES):
        scr[pl.ds(base + g, SUBLANES, stride=PERM_PITCH), :] = v3[g]
    return jnp.stack([scr[pl.ds(dyn_zero + (base + PERM_PITCH * r), SUBLANES), :]
                      for r in range(SUBLANES)], axis=0)


def _permute_chunk(scr, chunk, xs, dyn_zero):
    cols = []
    for l in range(xs.shape[1] // LANES):
        v3 = xs[:, l * LANES:(l + 1) * LANES].reshape(SUBLANES, SUBLANES, LANES)
        plane = scr.at[l % scr.shape[0]]
        cols.append(_permute_rows(plane, chunk, v3, dyn_zero).reshape(ROW_CHUNK, LANES))
    return jnp.concatenate(cols, axis=1)


def _causal_conv_permuted(vp3, prev_tail, w_ref, col):
    k_taps = w_ref.shape[0]
    n_prev = k_taps - 1
    tail = vp3[SUBLANES - n_prev:]
    sublane = lax.broadcasted_iota(jnp.int32, tail.shape, 1)
    before = jnp.where(sublane >= 1, pltpu.roll(tail, 1, axis=1), pltpu.roll(prev_tail, 1, axis=1))
    ext = jnp.concatenate([before, vp3], axis=0)
    acc = w_ref[k_taps - 1:k_taps, col:col + LANES] * vp3
    for j in range(n_prev):
        acc = acc + w_ref[j:j + 1, col:col + LANES] * ext[j:j + SUBLANES]
    return acc, tail


def _linear_scan_permuted(ap3, bp3, h):
    a_cum = [ap3[0]]
    b_cum = [bp3[0]]
    for r in range(1, SUBLANES):
        b_cum.append(ap3[r] * b_cum[-1] + bp3[r])
        a_cum.append(ap3[r] * a_cum[-1])
    alpha, beta = a_cum[-1], b_cum[-1]
    sublane = lax.broadcasted_iota(jnp.int32, alpha.shape, 0)
    for k in (1, 2, 4):
        keep = sublane >= k
        alpha_prev = jnp.where(keep, pltpu.roll(alpha, k, axis=0), 1.0)
        beta_prev = jnp.where(keep, pltpu.roll(beta, k, axis=0), 0.0)
        beta = alpha * beta_prev + beta
        alpha = alpha * alpha_prev
    after = alpha * h + beta
    entering = jnp.where(sublane >= 1, pltpu.roll(after, 1, axis=0), h)
    states = [a_cum[r] * entering + b_cum[r] for r in range(SUBLANES)]
    return jnp.stack(states, axis=0), after[SUBLANES - 1:SUBLANES, :]


def _layer_kernel(*refs, final_norm, seq_tiles, prep_next, permute_in, unpermute_out):
    refs = list(refs)
    w_raw_ref = refs.pop(2) if prep_next else None
    w_next_ref = refs.pop(17) if prep_next else None
    (x_ref, x_next_ref, mod_ref, gain_ref, w_in_ref, w_out_ref, conv_a_ref, sgu_w_ref,
     sgu_b_ref, lru_cw_ref, lru_cb_ref, lru_w_ref, lru_ba_ref, lru_bx_ref,
     lam_ref, fgain_ref, o_ref,
     h_scr, h_next_scr, m_scr, p_scr, xc_scr, xcb_scr, vn_scr, pre_scr, z_scr, sgu_wp_scr,
     perm_scr, cx_carry, rx_carry, h_carry) = refs
    (mod_ref, gain_ref, w_in_ref, w_out_ref, conv_a_ref, sgu_w_ref, sgu_b_ref, lru_cw_ref,
     lru_cb_ref, lru_w_ref, lru_ba_ref, lru_bx_ref, lam_ref) = (
        r.at[0] for r in (mod_ref, gain_ref, w_in_ref, w_out_ref, conv_a_ref, sgu_w_ref,
                          sgu_b_ref, lru_cw_ref, lru_cb_ref, lru_w_ref, lru_ba_ref, lru_bx_ref,
                          lam_ref))
    tile = x_ref.shape[1]
    rows = ROW_CHUNK
    n_row_chunks = tile // rows
    n_sgu_chunks = tile // SGU_CHUNK
    step = pl.program_id(0)
    dyn_zero = step // pl.num_programs(0)
    b = step // seq_tiles
    b_next = jnp.minimum(step + 1, pl.num_programs(0) - 1) // seq_tiles

    @pl.when(step % seq_tiles == 0)
    def _():
        cx_carry[...] = jnp.zeros_like(cx_carry)
        rx_carry[...] = jnp.zeros_like(rx_carry)
        h_carry[...] = jnp.zeros_like(h_carry)

    if prep_next:
        for jb in range(N_COL_BLOCKS):
            for k in range(N_SLICES):
                c0 = k * D_MODEL + jb * LANES
                w_next_ref[jb, :, k * LANES:(k + 1) * LANES] = (
                    w_raw_ref[0, :, c0:c0 + LANES].astype(jnp.bfloat16))

    def residual_chunk(src_ref, c):
        xs = src_ref[0, c * rows:(c + 1) * rows, :]
        return _permute_chunk(perm_scr, c, xs, dyn_zero) if permute_in else xs

    def modulated_norm(src_ref, batch, dst_ref):
        shift = mod_ref[0, pl.ds(batch, 1), :]
        scale = mod_ref[1, pl.ds(batch, 1), :]
        gain_scale = gain_ref[...] * (1.0 + scale)
        for c in range(n_row_chunks):
            r0 = c * rows
            xs = residual_chunk(src_ref, c)
            inv = lax.rsqrt(jnp.mean(xs * xs, axis=-1, keepdims=True) + EPS)
            dst_ref[r0:r0 + rows, :] = ((xs * inv) * gain_scale + shift).astype(jnp.bfloat16)

    def project(j, operand_ref, half):
        cols = slice(half * (BLOCK_COLS // 2), (half + 1) * (BLOCK_COLS // 2))
        p_scr[j % P_SLOTS, :, cols] = jnp.dot(operand_ref[...], w_in_ref[j, :, cols],
                                              preferred_element_type=jnp.float32)

    @pl.when(step == 0)
    def _():
        modulated_norm(x_ref, b, h_scr)
        project(0, h_scr, 0)
        project(0, h_scr, 1)

    modulated_norm(x_next_ref, b_next, h_next_scr)
    gate = mod_ref[2, pl.ds(b, 1), :]

    def position_of(row):
        return (row & -ROW_CHUNK) | ((row & (SUBLANES - 1)) * SUBLANES) | (
            (row // SUBLANES) & (SUBLANES - 1))

    @pl.when(step == 0)
    def _():
        row = lax.broadcasted_iota(jnp.int32, (SGU_CHUNK, SGU_CHUNK), 0)
        col = lax.broadcasted_iota(jnp.int32, (SGU_CHUNK, SGU_CHUNK), 1)
        select_rows = (col == position_of(row)).astype(jnp.bfloat16)
        select_cols = (row == position_of(col)).astype(jnp.bfloat16)
        for head in range(SGU_HEADS):
            w = jnp.where(row >= col, sgu_w_ref[head], 0.0).astype(jnp.bfloat16)
            w = jnp.dot(w, select_cols, preferred_element_type=jnp.float32).astype(jnp.bfloat16)
            sgu_wp_scr[head] = jnp.dot(select_rows, w, preferred_element_type=jnp.float32
                                       ).astype(jnp.bfloat16)

    def mix(j):
        col = j * LANES
        lanes = slice(col, col + LANES)
        p = p_scr.at[j % P_SLOTS]
        xc_s, xcb_s, vn_s, pre_s, z_s = (
            r.at[j % 2] for r in (xc_scr, xcb_scr, vn_scr, pre_scr, z_scr))

        def sl(k, r0):
            return p[r0:r0 + rows, k * LANES:(k + 1) * LANES]

        def sl3(k, r0):
            return sl(k, r0).reshape(SUBLANES, SUBLANES, LANES)

        rx_tail = rx_carry[:, :, lanes]
        for c in range(n_row_chunks):
            r0 = c * rows
            xcp, rx_tail = _causal_conv_permuted(sl3(R_X, r0), rx_tail, lru_cw_ref, col)
            xc = (xcp + lru_cb_ref[:, lanes]).reshape(rows, LANES)
            xc_s[r0:r0 + rows, :] = xc
            xcb_s[r0:r0 + rows, :] = xc.astype(jnp.bfloat16)
            v = sl(S_V, r0)
            vc = v - jnp.mean(v, axis=-1, keepdims=True)
            var = jnp.mean(vc * vc, axis=-1, keepdims=True)
            vn_s[r0:r0 + rows, :] = (vc * lax.rsqrt(var + EPS)).astype(jnp.bfloat16)
        rx_carry[:, :, lanes] = rx_tail

        pre_s[...] = jnp.dot(xcb_s[...], lru_w_ref[j], preferred_element_type=jnp.float32)
        vn_wide = jnp.concatenate(
            [vn_s[n * SGU_CHUNK:(n + 1) * SGU_CHUNK, :] for n in range(n_sgu_chunks)], axis=1)
        z_s[...] = jnp.dot(sgu_wp_scr[j], vn_wide, preferred_element_type=jnp.float32)
        yield

        lam = lam_ref[:, lanes]
        softplus_neg_lam = jnp.maximum(-lam, 0.0) + jnp.log1p(jnp.exp(-jnp.abs(lam)))
        log2_a_per_r = (-LRU_C * LOG2_E) * softplus_neg_lam
        cx_tail = cx_carry[:, :, lanes]
        state = h_carry[:, lanes]
        for c in range(n_row_chunks):
            r0 = c * rows
            chunk, q0 = divmod(r0, SGU_CHUNK)

            convp, cx_tail = _causal_conv_permuted(sl3(A_C, r0) * sl3(A_X, r0), cx_tail,
                                                   conv_a_ref, col)
            conv = convp.reshape(rows, LANES)
            merged = _sigmoid(sl(G_A, r0)) * ((_silu(sl(A_Z, r0)) * sl(A_B, r0)) * conv)

            z = (z_s[q0:q0 + rows, chunk * LANES:(chunk + 1) * LANES]
                 + sgu_b_ref[j, q0:q0 + rows, :])
            merged = merged + _sigmoid(sl(G_S, r0)) * ((_silu(sl(S_Z, r0)) * sl(S_U, r0)) * z)

            r =_sigmoid(pre_s[r0:r0 + rows, :LANES] + lru_ba_ref[:, lanes])
            i = _sigmoid(pre_s[r0:r0 + rows, LANES:] + lru_bx_ref[:, lanes])
            a = jnp.exp2(r * log2_a_per_r)
            y = 1.0 - a * a
            bb = (y * lax.rsqrt(jnp.maximum(y, 1e-30))) * (i * xc_s[r0:r0 + rows, :])
            hsp, state = _linear_scan_permuted(a.reshape(SUBLANES, SUBLANES, LANES),
                                               bb.reshape(SUBLANES, SUBLANES, LANES), state)
            merged = merged + _sigmoid(sl(G_R, r0)) * (
                _silu(sl(R_Z, r0)) * hsp.reshape(rows, LANES))

            m_scr[r0:r0 + rows, lanes] = merged.astype(jnp.bfloat16)
        cx_carry[:, :, lanes] = cx_tail
        h_carry[:, lanes] = state

    for j in range(N_COL_BLOCKS):
        upcoming = (j + 1, h_scr) if j + 1 < N_COL_BLOCKS else (0, h_next_scr)
        mix_parts = mix(j)
        project(*upcoming, 0)
        next(mix_parts)
        project(*upcoming, 1)
        next(mix_parts, None)

    res = p_scr.at[(N_COL_BLOCKS - 1) % P_SLOTS]
    res[:, :D_MODEL] = jnp.dot(m_scr[...], w_out_ref[...], preferred_element_type=jnp.float32)
    for c in range(n_row_chunks):
        r0 = c * rows
        out = residual_chunk(x_ref, c) + gate * res[r0:r0 + rows, :D_MODEL]
        if final_norm:
            inv_o = lax.rsqrt(jnp.mean(out * out, axis=-1, keepdims=True) + EPS)
            out = (out * inv_o) * fgain_ref[...]
        if unpermute_out:
            out = _permute_chunk(perm_scr, c, out, dyn_zero)
        o_ref[0, r0:r0 + rows, :] = out
    h_scr[...] = h_next_scr[...]


def _layer_resident(arr, layer):
    index = (layer,) + (0,) * (arr.ndim - 1)
    return pl.BlockSpec((1,) + arr.shape[1:], lambda i: index, pipeline_mode=pl.Buffered(1))


def _tile_index(i, *, seq_tiles, last, ahead):
    t = jnp.minimum(i + ahead, last)
    return (t // seq_tiles, t % seq_tiles, 0)


def _layer(x, layer, stacked, w_in_blocked, w_in_raw, final_gain, *, last):
    bsz, seq, d = x.shape
    tile = SEQ_TILE
    seq_tiles = seq // tile
    n_steps = bsz * seq_tiles
    prep_next = layer + 1 < w_in_raw.shape[0]
    slab = d // n_steps
    tile_spec = functools.partial(_tile_index, seq_tiles=seq_tiles, last=n_steps - 1)
    operands =[(arr, layer) for arr in stacked]
    operands.insert(2, (w_in_blocked[None], 0))
    in_specs = [pl.BlockSpec((1, tile, d), functools.partial(tile_spec, ahead=0)),
                pl.BlockSpec((1, tile, d), functools.partial(tile_spec, ahead=1))]
    inputs = [x, x]
    out_specs = [pl.BlockSpec((1, tile, d), functools.partial(tile_spec, ahead=0))]
    out_shape = [jax.ShapeDtypeStruct(x.shape, x.dtype)]
    if prep_next:
        in_specs.append(pl.BlockSpec((1, slab, w_in_raw.shape[2]), lambda i: (layer + 1, i, 0)))
        inputs.append(w_in_raw)
        out_specs.append(pl.BlockSpec((N_COL_BLOCKS, slab, BLOCK_COLS), lambda i: (0, i, 0)))
        out_shape.append(jax.ShapeDtypeStruct((N_COL_BLOCKS, d, BLOCK_COLS), jnp.bfloat16))
    in_specs += [_layer_resident(arr, index) for arr, index in operands]
    in_specs.append(pl.BlockSpec(final_gain.shape, lambda i: (0, 0), pipeline_mode=pl.Buffered(1)))
    inputs += [arr for arr, _ in operands] + [final_gain]
    outs = pl.pallas_call(
        functools.partial(_layer_kernel, final_norm=last, seq_tiles=seq_tiles,
                          prep_next=prep_next, permute_in=(layer == 0), unpermute_out=last),
        grid=(n_steps,),
        in_specs=in_specs,
        out_specs=out_specs,
        out_shape=out_shape,
        scratch_shapes=[
            pltpu.VMEM((tile, d), jnp.bfloat16),
            pltpu.VMEM((tile, d), jnp.bfloat16),
            pltpu.VMEM((tile, d), jnp.bfloat16),
            pltpu.VMEM((P_SLOTS, tile, BLOCK_COLS), jnp.float32),
            pltpu.VMEM((2, tile, LANES), jnp.float32),
            pltpu.VMEM((2, tile, LANES), jnp.bfloat16),
            pltpu.VMEM((2, tile, LANES), jnp.bfloat16),
            pltpu.VMEM((2, tile, 2 * LANES), jnp.float32),
            pltpu.VMEM((2, SGU_CHUNK, tile), jnp.float32),
            pltpu.VMEM((SGU_HEADS, SGU_CHUNK, SGU_CHUNK), jnp.bfloat16),
            pltpu.VMEM((2, tile // ROW_CHUNK * PERM_ROWS, LANES), jnp.float32),
            pltpu.VMEM((CONV_K - 1, SUBLANES, d), jnp.float32),
            pltpu.VMEM((LRU_CONV_K - 1, SUBLANES, d), jnp.float32),
            pltpu.VMEM((1, d), jnp.float32),
        ],
        compiler_params=pltpu.CompilerParams(
            dimension_semantics=("arbitrary",),
            vmem_limit_bytes=VMEM_LIMIT_BYTES),
        name="hybrid_mixer_layer",
    )(*inputs)
    return (outs[0], outs[1]) if prep_next else (outs[0], None)


def _block_columns_kernel(*refs):
    o_ref = refs[-1]
    for k in range(N_SLICES):
        o_ref[0, :, k * LANES:(k + 1) * LANES] = refs[k][0].astype(jnp.bfloat16)


def _slice_block_index(j, *, layer, k):
    return (layer, 0, k * N_COL_BLOCKS + j)


def _block_columns(w_in, layer):
    _, d, _ = w_in.shape
    return pl.pallas_call(
        _block_columns_kernel,
        grid=(N_COL_BLOCKS,),
        in_specs=[pl.BlockSpec((1, d, LANES), functools.partial(_slice_block_index, layer=layer, k=k))
                  for k in range(N_SLICES)],
        out_specs=pl.BlockSpec((1, d, BLOCK_COLS), lambda j: (j, 0, 0)),
        out_shape=jax.ShapeDtypeStruct((N_COL_BLOCKS, d, BLOCK_COLS), jnp.bfloat16),
        name="block_columns_bf16",
    )(*([w_in] * N_SLICES))


def _permute_positions(arr, axis):
    axis = axis % arr.ndim
    shape = arr.shape
    split = arr.reshape(shape[:axis] + (SGU_CHUNK // ROW_CHUNK, SUBLANES, SUBLANES) + shape[axis + 1:])
    return jnp.swapaxes(split, axis + 1, axis + 2).reshape(shape)


def _pair_block_diag(w_a, w_x):
    def pairs(w):
        w = w.reshape(w.shape[0], LRU_HEADS // 2, 2, LRU_HEAD_DIM, LRU_HEAD_DIM)
        zero = jnp.zeros_like(w[:, :, 0])
        top = jnp.concatenate([w[:, :, 0], zero], axis=-1)
        bottom = jnp.concatenate([zero, w[:, :, 1]], axis=-1)
        return jnp.concatenate([top, bottom], axis=-2)
    return jnp.concatenate([pairs(w_a), pairs(w_x)], axis=-1)


def kernel(x, c, norm_gain, w_mod, b_mod, w_in, w_out, conv_a_w, sgu_w, sgu_b, lru_conv_w,
           lru_conv_b, lru_wa, lru_ba, lru_wx, lru_bx, lru_lambda, final_gain):
    depth = w_in.shape[0]
    d = x.shape[-1]
    stacked = (
        _modulation(c, w_mod, b_mod),
        norm_gain.reshape(depth, 1, d),
        w_out.astype(jnp.bfloat16),
        conv_a_w,
        sgu_w,
        jnp.broadcast_to(_permute_positions(sgu_b, -1)[..., None], sgu_b.shape + (LANES,)),
        lru_conv_w,
        lru_conv_b.reshape(depth, 1, d),
        _pair_block_diag(lru_wa, lru_wx).astype(jnp.bfloat16),
        lru_ba.reshape(depth, 1, d),
        lru_bx.reshape(depth, 1, d),
        lru_lambda.reshape(depth, 1, d),
    )
    w_in_blocked = _block_columns(w_in, 0)
    for l in range(depth):
        x, w_in_blocked = _layer(x, l, stacked, w_in_blocked, w_in, final_gain.reshape(1, d),
                                 last=(l == depth - 1))
    return x
```

```python
import functools
import math

import jax
import jax.numpy as jnp
from jax import lax
from jax.experimental import pallas as pl
from jax.experimental.pallas import tpu as pltpu

D_MODEL = 1024
SGU_CHUNK = 128
SGU_HEADS = 8
LRU_HEADS = 16
LRU_HEAD_DIM = 64
CONV_K = 3
LRU_CONV_K = 4
LRU_C = 8.0
EPS = 1e-6
LOG2_E = math.log2(math.e)

SUBLANES = 8
LANES = 128
N_SLICES = 12
(A_X, A_B, A_C, A_Z, S_U, S_V, S_Z, R_X, R_Z, G_A, G_S, G_R) = range(N_SLICES)
N_COL_BLOCKS = D_MODEL // LANES
BLOCK_COLS = N_SLICES * LANES
SEQ_TILE = 512
P_SLOTS = 2
ROW_CHUNK = SUBLANES * SUBLANES
PERM_PITCH = SUBLANES + 1
PERM_ROWS = SUBLANES * PERM_PITCH
VMEM_LIMIT_BYTES = 60 * 1024 * 1024


def _sigmoid(v):
    return 1.0 / (1.0 + jnp.exp2(v * (-LOG2_E)))


def _silu(v):
    return v * _sigmoid(v)


def _mod_kernel(c_ref, w_ref, b_ref, o_ref):
    c = c_ref[...]
    o_ref[0, 0] = jnp.dot(_silu(c), w_ref[0], preferred_element_type=jnp.float32) + b_ref[0, 0]


def _modulation(c, w_mod, b_mod):
    depth, d, _ = w_mod.shape
    bsz = c.shape[0]
    return pl.pallas_call(
        _mod_kernel,
        grid=(depth, 3),
        in_specs=[
            pl.BlockSpec((bsz, d), lambda l, k: (0, 0)),
            pl.BlockSpec((1, d, d), lambda l, k: (l, 0, k)),
            pl.BlockSpec((1, 1, 1, d), lambda l, k: (l, k, 0, 0)),
        ],
        out_specs=pl.BlockSpec((1, 1, bsz, d), lambda l, k: (l, k, 0, 0)),
        out_shape=jax.ShapeDtypeStruct((depth, 3, bsz, d), jnp.float32),
        name="adaln_modulation",
    )(c, w_mod, b_mod.reshape(depth, 3, 1, d))


def _permute_rows(scr, chunk, v3, dyn_zero):
    base = chunk * PERM_ROWS
    for g in range(SUBLANES):
        scr[pl.ds(base + g, SUBLANES, stride=PERM_PITCH), :] = v3[g]
    return jnp.stack([scr[pl.ds(dyn_zero + (base + PERM_PITCH * r), SUBLANES), :]
                      for r in range(SUBLANES)], axis=0)


def _permute_chunk(scr, chunk, xs, dyn_zero):
    cols = []
    for l in range(xs.shape[1] // LANES):
        v3 = xs[:, l * LANES:(l + 1) * LANES].reshape(SUBLANES, SUBLANES, LANES)
        plane = scr.at[l % scr.shape[0]]
        cols.append(_permute_rows(plane, chunk, v3, dyn_zero).reshape(ROW_CHUNK, LANES))
    return jnp.concatenate(cols, axis=1)


def _causal_conv_permuted(vp3, prev_tail, w_ref, col):
    k_taps = w_ref.shape[0]
    n_prev = k_taps - 1
    tail = vp3[SUBLANES - n_prev:]
    sublane = lax.broadcasted_iota(jnp.int32, tail.shape, 1)
    before = jnp.where(sublane >= 1, pltpu.roll(tail, 1, axis=1), pltpu.roll(prev_tail, 1, axis=1))
    ext = jnp.concatenate([before, vp3], axis=0)
    acc = w_ref[k_taps - 1:k_taps, col:col + LANES] * vp3
    for j in range(n_prev):
        acc = acc + w_ref[j:j + 1, col:col + LANES] * ext[j:j + SUBLANES]
    return acc, tail


def _linear_scan_permuted(ap3, bp3, h):
    a_cum = [ap3[0]]
    b_cum = [bp3[0]]
    for r in range(1, SUBLANES):
        b_cum.append(ap3[r] * b_cum[-1] + bp3[r])
        a_cum.append(ap3[r] * a_cum[-1])
    alpha, beta = a_cum[-1], b_cum[-1]
    sublane = lax.broadcasted_iota(jnp.int32, alpha.shape, 0)
    for k in (1, 2, 4):
        keep = sublane >= k
        alpha_prev = jnp.where(keep, pltpu.roll(alpha, k, axis=0), 1.0)
        beta_prev = jnp.where(keep, pltpu.roll(beta, k, axis=0), 0.0)
        beta = alpha * beta_prev + beta
        alpha = alpha * alpha_prev
    after = alpha * h + beta
    entering = jnp.where(sublane >= 1, pltpu.roll(after, 1, axis=0), h)
    states = [a_cum[r] * entering + b_cum[r] for r in range(SUBLANES)]
    return jnp.stack(states, axis=0), after[SUBLANES - 1:SUBLANES, :]


def _layer_kernel(*refs, final_norm, seq_tiles, prep_next, permute_in, unpermute_out):
    refs = list(refs)
    w_raw_ref = refs.pop(2) if prep_next else None
    w_next_ref = refs.pop(17) if prep_next else None
    (x_ref, x_next_ref, mod_ref, gain_ref, w_in_ref, w_out_ref, conv_a_ref, sgu_w_ref,
     sgu_b_ref, lru_cw_ref, lru_cb_ref, lru_w_ref, lru_ba_ref, lru_bx_ref,
     lam_ref, fgain_ref, o_ref,
     h_scr, h_next_scr, m_scr, p_scr, xc_scr, xcb_scr, vn_scr, pre_scr, z_scr, sgu_wp_scr,
     perm_scr, cx_carry, rx_carry, h_carry) = refs
    (mod_ref, gain_ref, w_in_ref, w_out_ref, conv_a_ref, sgu_w_ref, sgu_b_ref, lru_cw_ref,
     lru_cb_ref, lru_w_ref, lru_ba_ref, lru_bx_ref, lam_ref) = (
        r.at[0] for r in (mod_ref, gain_ref, w_in_ref, w_out_ref, conv_a_ref, sgu_w_ref,
                          sgu_b_ref, lru_cw_ref, lru_cb_ref, lru_w_ref, lru_ba_ref, lru_bx_ref,
                          lam_ref))
    tile = x_ref.shape[1]
    rows = ROW_CHUNK
    n_row_chunks = tile // rows
    n_sgu_chunks = tile // SGU_CHUNK
    step = pl.program_id(0)
    dyn_zero = step // pl.num_programs(0)
    b = step // seq_tiles
    b_next = jnp.minimum(step + 1, pl.num_programs(0) - 1) // seq_tiles

    @pl.when(step % seq_tiles == 0)
    def _():
        cx_carry[...] = jnp.zeros_like(cx_carry)
        rx_carry[...] = jnp.zeros_like(rx_carry)
        h_carry[...] = jnp.zeros_like(h_carry)

    if prep_next:
        for jb in range(N_COL_BLOCKS):
            for k in range(N_SLICES):
                c0 = k * D_MODEL + jb * LANES
                w_next_ref[jb, :, k * LANES:(k + 1) * LANES] = (
                    w_raw_ref[0, :, c0:c0 + LANES].astype(jnp.bfloat16))

    def residual_chunk(src_ref, c):
        xs = src_ref[0, c * rows:(c + 1) * rows, :]
        return _permute_chunk(perm_scr, c, xs, dyn_zero) if permute_in else xs

    def modulated_norm(src_ref, batch, dst_ref):
        shift = mod_ref[0, pl.ds(batch, 1), :]
        scale = mod_ref[1, pl.ds(batch, 1), :]
        gain_scale = gain_ref[...] * (1.0 + scale)
        for c in range(n_row_chunks):
            r0 = c * rows
            xs = residual_chunk(src_ref, c)
            inv = lax.rsqrt(jnp.mean(xs * xs, axis=-1, keepdims=True) + EPS)
            dst_ref[r0:r0 + rows, :] = ((xs * inv) * gain_scale + shift).astype(jnp.bfloat16)

    def project(j, operand_ref, half):
        cols = slice(half * (BLOCK_COLS // 2), (half + 1) * (BLOCK_COLS // 2))
        p_scr[j % P_SLOTS, :, cols] = jnp.dot(operand_ref[...], w_in_ref[j, :, cols],
                                              preferred_element_type=jnp.float32)

    def position_of(row):
        return (row & -ROW_CHUNK) | ((row & (SUBLANES - 1)) * SUBLANES) | (
            (row // SUBLANES) & (SUBLANES - 1))

    @pl.when(step == 0)
    def _():
        modulated_norm(x_ref, b, h_scr)
        project(0, h_scr, 0)
        project(0, h_scr, 1)
        row = lax.broadcasted_iota(jnp.int32, (SGU_CHUNK, SGU_CHUNK), 0)
        col = lax.broadcasted_iota(jnp.int32, (SGU_CHUNK, SGU_CHUNK), 1)
        select_rows = (col == position_of(row)).astype(jnp.bfloat16)
        select_cols = (row == position_of(col)).astype(jnp.bfloat16)
        for head in range(SGU_HEADS):
            w = jnp.where(row >= col, sgu_w_ref[head], 0.0).astype(jnp.bfloat16)
            w = jnp.dot(w, select_cols, preferred_element_type=jnp.float32).astype(jnp.bfloat16)
            sgu_wp_scr[head] = jnp.dot(select_rows, w, preferred_element_type=jnp.float32
                                       ).astype(jnp.bfloat16)

    modulated_norm(x_next_ref, b_next, h_next_scr)
    gate = mod_ref[2, pl.ds(b, 1), :]

    def mix(j):
        col = j * LANES
        lanes = slice(col, col + LANES)
        p = p_scr.at[j % P_SLOTS]
        xc_s, xcb_s, vn_s, pre_s, z_s = (
            r.at[j % 2] for r in (xc_scr, xcb_scr, vn_scr, pre_scr, z_scr))

        def sl(k, r0):
            return p[r0:r0 + rows, k * LANES:(k + 1) * LANES]

        def sl3(k, r0):
            return sl(k, r0).reshape(SUBLANES, SUBLANES, LANES)

        rx_tail = rx_carry[:, :, lanes]
        for c in range(n_row_chunks):
            r0 = c * rows
            xcp, rx_tail = _causal_conv_permuted(sl3(R_X, r0), rx_tail, lru_cw_ref, col)
            xc = (xcp + lru_cb_ref[:, lanes]).reshape(rows, LANES)
            xc_s[r0:r0 + rows, :] = xc
            xcb_s[r0:r0 + rows, :] = xc.astype(jnp.bfloat16)
            v = sl(S_V, r0)
            vc = v - jnp.mean(v, axis=-1, keepdims=True)
            var = jnp.mean(vc * vc, axis=-1, keepdims=True)
            vn_s[r0:r0 + rows, :] = (vc * lax.rsqrt(var + EPS)).astype(jnp.bfloat16)
        rx_carry[:, :, lanes] = rx_tail

        pre_s[...] = jnp.dot(xcb_s[...], lru_w_ref[j], preferred_element_type=jnp.float32)
        vn_wide = jnp.concatenate(
            [vn_s[n * SGU_CHUNK:(n + 1) * SGU_CHUNK, :] for n in range(n_sgu_chunks)], axis=1)
        z_s[...] = jnp.dot(sgu_wp_scr[j], vn_wide, preferred_element_type=jnp.float32)
        yield

        lam = lam_ref[:, lanes]
        softplus_neg_lam = jnp.maximum(-lam, 0.0) + jnp.log1p(jnp.exp(-jnp.abs(lam)))
        log2_a_per_r = (-LRU_C * LOG2_E) * softplus_neg_lam
        cx_tail = cx_carry[:, :, lanes]
        state = h_carry[:, lanes]
        for c in range(n_row_chunks):
            r0 = c * rows
            chunk, q0 = divmod(r0, SGU_CHUNK)

            convp, cx_tail = _causal_conv_permuted(sl3(A_C, r0) * sl3(A_X, r0), cx_tail,
                                                   conv_a_ref, col)
            conv = convp.reshape(rows, LANES)
            merged = _sigmoid(sl(G_A, r0)) * ((_silu(sl(A_Z, r0)) * sl(A_B, r0)) * conv)

            z = (z_s[q0:q0 + rows, chunk * LANES:(chunk + 1) * LANES]
                 + sgu_b_ref[j, q0:q0 + rows, :])
            merged = merged + _sigmoid(sl(G_S, r0)) * ((_silu(sl(S_Z, r0)) * sl(S_U, r0)) * z)

            r =_sigmoid(pre_s[r0:r0 + rows, :LANES] + lru_ba_ref[:, lanes])
            i = _sigmoid(pre_s[r0:r0 + rows, LANES:] + lru_bx_ref[:, lanes])
            a = jnp.exp2(r * log2_a_per_r)
            y = 1.0 - a * a
            bb = (y * lax.rsqrt(jnp.maximum(y, 1e-30))) * (i * xc_s[r0:r0 + rows, :])
            hsp, state = _linear_scan_permuted(a.reshape(SUBLANES, SUBLANES, LANES),
                                               bb.reshape(SUBLANES, SUBLANES, LANES), state)
            merged = merged + _sigmoid(sl(G_R, r0)) * (
                _silu(sl(R_Z, r0)) * hsp.reshape(rows, LANES))

            m_scr[r0:r0 + rows, lanes] = merged.astype(jnp.bfloat16)
        cx_carry[:, :, lanes] = cx_tail
        h_carry[:, lanes] = state

    for j in range(N_COL_BLOCKS):
        upcoming = (j + 1, h_scr) if j + 1 < N_COL_BLOCKS else (0, h_next_scr)
        mix_parts = mix(j)
        project(*upcoming, 0)
        next(mix_parts)
        project(*upcoming, 1)
        next(mix_parts, None)

    res = p_scr.at[(N_COL_BLOCKS - 1) % P_SLOTS]
    res[:, :D_MODEL] = jnp.dot(m_scr[...], w_out_ref[...], preferred_element_type=jnp.float32)
    for c in range(n_row_chunks):
        r0 = c * rows
        out = residual_chunk(x_ref, c) + gate * res[r0:r0 + rows, :D_MODEL]
        if final_norm:
            inv_o = lax.rsqrt(jnp.mean(out * out, axis=-1, keepdims=True) + EPS)
            out = (out * inv_o) * fgain_ref[...]
        if unpermute_out:
            out = _permute_chunk(perm_scr, c, out, dyn_zero)
        o_ref[0, r0:r0 + rows, :] = out
    h_scr[...] = h_next_scr[...]


def _layer_resident(arr, layer):
    index = (layer,) + (0,) * (arr.ndim - 1)
    return pl.BlockSpec((1,) + arr.shape[1:], lambda i: index, pipeline_mode=pl.Buffered(1))


def _tile_index(i, *, seq_tiles, last, ahead):
    t = jnp.minimum(i + ahead, last)
    return (t // seq_tiles, t % seq_tiles, 0)


def _layer(x, layer, stacked, w_in_blocked, w_in_raw, final_gain, *, last):
    bsz, seq, d = x.shape
    tile = SEQ_TILE
    seq_tiles = seq // tile
    n_steps = bsz * seq_tiles
    prep_next = layer + 1 < w_in_raw.shape[0]
    slab = d // n_steps
    tile_spec = functools.partial(_tile_index, seq_tiles=seq_tiles, last=n_steps - 1)
    operands =[(arr, layer) for arr in stacked]
    operands.insert(2, (w_in_blocked[None], 0))
    in_specs = [pl.BlockSpec((1, tile, d), functools.partial(tile_spec, ahead=0)),
                pl.BlockSpec((1, tile, d), functools.partial(tile_spec, ahead=1))]
    inputs = [x, x]
    out_specs = [pl.BlockSpec((1, tile, d), functools.partial(tile_spec, ahead=0))]
    out_shape = [jax.ShapeDtypeStruct(x.shape, x.dtype)]
    if prep_next:
        in_specs.append(pl.BlockSpec((1, slab, w_in_raw.shape[2]), lambda i: (layer + 1, i, 0)))
        inputs.append(w_in_raw)
        out_specs.append(pl.BlockSpec((N_COL_BLOCKS, slab, BLOCK_COLS), lambda i: (0, i, 0)))
        out_shape.append(jax.ShapeDtypeStruct((N_COL_BLOCKS, d, BLOCK_COLS), jnp.bfloat16))
    in_specs += [_layer_resident(arr, index) for arr, index in operands]
    in_specs.append(pl.BlockSpec(final_gain.shape, lambda i: (0, 0), pipeline_mode=pl.Buffered(1)))
    inputs += [arr for arr, _ in operands] + [final_gain]
    outs = pl.pallas_call(
        functools.partial(_layer_kernel, final_norm=last, seq_tiles=seq_tiles,
                          prep_next=prep_next, permute_in=(layer == 0), unpermute_out=last),
        grid=(n_steps,),
        in_specs=in_specs,
        out_specs=out_specs,
        out_shape=out_shape,
        scratch_shapes=[
            pltpu.VMEM((tile, d), jnp.bfloat16),
            pltpu.VMEM((tile, d), jnp.bfloat16),
            pltpu.VMEM((tile, d), jnp.bfloat16),
            pltpu.VMEM((P_SLOTS, tile, BLOCK_COLS), jnp.float32),
            pltpu.VMEM((2, tile, LANES), jnp.float32),
            pltpu.VMEM((2, tile, LANES), jnp.bfloat16),
            pltpu.VMEM((2, tile, LANES), jnp.bfloat16),
            pltpu.VMEM((2, tile, 2 * LANES), jnp.float32),
            pltpu.VMEM((2, SGU_CHUNK, tile), jnp.float32),
            pltpu.VMEM((SGU_HEADS, SGU_CHUNK, SGU_CHUNK), jnp.bfloat16),
            pltpu.VMEM((2, tile // ROW_CHUNK * PERM_ROWS, LANES), jnp.float32),
            pltpu.VMEM((CONV_K - 1, SUBLANES, d), jnp.float32),
            pltpu.VMEM((LRU_CONV_K - 1, SUBLANES, d), jnp.float32),
            pltpu.VMEM((1, d), jnp.float32),
        ],
        compiler_params=pltpu.CompilerParams(
            dimension_semantics=("arbitrary",),
            vmem_limit_bytes=VMEM_LIMIT_BYTES),
        name="hybrid_mixer_layer",
    )(*inputs)
    return (outs[0], outs[1]) if prep_next else (outs[0], None)


def _block_columns_kernel(*refs):
    o_ref = refs[-1]
    for k in range(N_SLICES):
        o_ref[0, :, k * LANES:(k + 1) * LANES] = refs[k][0].astype(jnp.bfloat16)


def _slice_block_index(j, *, layer, k):
    return (layer, 0, k * N_COL_BLOCKS + j)


def _block_columns(w_in, layer):
    _, d, _ = w_in.shape
    return pl.pallas_call(
        _block_columns_kernel,
        grid=(N_COL_BLOCKS,),
        in_specs=[pl.BlockSpec((1, d, LANES), functools.partial(_slice_block_index, layer=layer, k=k))
                  for k in range(N_SLICES)],
        out_specs=pl.BlockSpec((1, d, BLOCK_COLS), lambda j: (j, 0, 0)),
        out_shape=jax.ShapeDtypeStruct((N_COL_BLOCKS, d, BLOCK_COLS), jnp.bfloat16),
        name="block_columns_bf16",
    )(*([w_in] * N_SLICES))


def _permute_positions(arr, axis):
    axis = axis % arr.ndim
    shape = arr.shape
    split = arr.reshape(shape[:axis] + (SGU_CHUNK // ROW_CHUNK, SUBLANES, SUBLANES) + shape[axis + 1:])
    return jnp.swapaxes(split, axis + 1, axis + 2).reshape(shape)


def _pair_block_diag(w_a, w_x):
    def pairs(w):
        w = w.reshape(w.shape[0], LRU_HEADS // 2, 2, LRU_HEAD_DIM, LRU_HEAD_DIM)
        zero = jnp.zeros_like(w[:, :, 0])
        top = jnp.concatenate([w[:, :, 0], zero], axis=-1)
        bottom = jnp.concatenate([zero, w[:, :, 1]], axis=-1)
        return jnp.concatenate([top, bottom], axis=-2)
    return jnp.concatenate([pairs(w_a), pairs(w_x)], axis=-1)


def kernel(x, c, norm_gain, w_mod, b_mod, w_in, w_out, conv_a_w, sgu_w, sgu_b, lru_conv_w,
           lru_conv_b, lru_wa, lru_ba, lru_wx, lru_bx, lru_lambda, final_gain):
    depth = w_in.shape[0]
    d = x.shape[-1]
    stacked = (
        _modulation(c, w_mod, b_mod),
        norm_gain.reshape(depth, 1, d),
        w_out.astype(jnp.bfloat16),
        conv_a_w,
        sgu_w,
        jnp.broadcast_to(_permute_positions(sgu_b, -1)[..., None], sgu_b.shape + (LANES,)),
        lru_conv_w,
        lru_conv_b.reshape(depth, 1, d),
        _pair_block_diag(lru_wa, lru_wx).astype(jnp.bfloat16),
        lru_ba.reshape(depth, 1, d),
        lru_bx.reshape(depth, 1, d),
        lru_lambda.reshape(depth, 1, d),
    )
    w_in_blocked = _block_columns(w_in, 0)
    for l in range(depth):
        x, w_in_blocked = _layer(x, l, stacked, w_in_blocked, w_in, final_gain.reshape(1, d),
                                 last=(l == depth - 1))
    return x
```

```python
import functools
import math

import jax
import jax.numpy as jnp
from jax import lax
from jax.experimental import pallas as pl
from jax.experimental.pallas import tpu as pltpu

D_MODEL = 1024
SGU_CHUNK = 128
SGU_HEADS = 8
LRU_HEADS = 16
LRU_HEAD_DIM = 64
CONV_K = 3
LRU_CONV_K = 4
LRU_C = 8.0
EPS = 1e-6
LOG2_E = math.log2(math.e)

SUBLANES = 8
LANES = 128
N_SLICES = 12
(A_X, A_B, A_C, A_Z, S_U, S_V, S_Z, R_X, R_Z, G_A, G_S, G_R) = range(N_SLICES)
N_COL_BLOCKS = D_MODEL // LANES
BLOCK_COLS = N_SLICES * LANES
SEQ_TILE = 512
P_SLOTS = 2
ROW_CHUNK = SUBLANES * SUBLANES
PERM_PITCH = SUBLANES + 1
PERM_ROWS = SUBLANES * PERM_PITCH
VMEM_LIMIT_BYTES = 60 * 1024 * 1024


def _sigmoid(v):
    return 1.0 / (1.0 + jnp.exp2(v * (-LOG2_E)))


def _silu(v):
    return v * _sigmoid(v)


def _mod_kernel(c_ref, w_ref, b_ref, o_ref):
    c = c_ref[...]
    o_ref[0, 0] = jnp.dot(_silu(c), w_ref[0], preferred_element_type=jnp.float32) + b_ref[0, 0]


def _modulation(c, w_mod, b_mod):
    depth, d, _ = w_mod.shape
    bsz = c.shape[0]
    return pl.pallas_call(
        _mod_kernel,
        grid=(depth, 3),
        in_specs=[
            pl.BlockSpec((bsz, d), lambda l, k: (0, 0)),
            pl.BlockSpec((1, d, d), lambda l, k: (l, 0, k)),
            pl.BlockSpec((1, 1, 1, d), lambda l, k: (l, k, 0, 0)),
        ],
        out_specs=pl.BlockSpec((1, 1, bsz, d), lambda l, k: (l, k, 0, 0)),
        out_shape=jax.ShapeDtypeStruct((depth, 3, bsz, d), jnp.float32),
        name="adaln_modulation",
    )(c, w_mod, b_mod.reshape(depth, 3, 1, d))


def _permute_rows(scr, chunk, v3, dyn_zero):
    base = chunk * PERM_ROWS
    for g in range(SUBLANES):
        scr[pl.ds(base + g, SUBLANES, stride=PERM_PITCH), :] = v3[g]
    return jnp.stack([scr[pl.ds(dyn_zero + (base + PERM_PITCH * r), SUBLANES), :]
                      for r in range(SUBLANES)], axis=0)


def _permute_chunk(scr, chunk, xs, dyn_zero):
    cols = []
    for l in range(xs.shape[1] // LANES):
        v3 = xs[:, l * LANES:(l + 1) * LANES].reshape(SUBLANES, SUBLANES, LANES)
        plane = scr.at[l % scr.shape[0]]
        cols.append(_permute_rows(plane, chunk, v3, dyn_zero).reshape(ROW_CHUNK, LANES))
    return jnp.concatenate(cols, axis=1)


def _causal_conv_permuted(vp3, prev_tail, w_ref, col):
    k_taps = w_ref.shape[0]
    n_prev = k_taps - 1
    tail = vp3[SUBLANES - n_prev:]
    sublane = lax.broadcasted_iota(jnp.int32, tail.shape, 1)
    before = jnp.where(sublane >= 1, pltpu.roll(tail, 1, axis=1), pltpu.roll(prev_tail, 1, axis=1))
    ext = jnp.concatenate([before, vp3], axis=0)
    acc = w_ref[k_taps - 1:k_taps, col:col + LANES] * vp3
    for j in range(n_prev):
        acc = acc + w_ref[j:j + 1, col:col + LANES] * ext[j:j + SUBLANES]
    return acc, tail


def _linear_scan_permuted(ap3, bp3, h):
    a_cum = [ap3[0]]
    b_cum = [bp3[0]]
    for r in range(1, SUBLANES):
        b_cum.append(ap3[r] * b_cum[-1] + bp3[r])
        a_cum.append(ap3[r] * a_cum[-1])
    alpha, beta = a_cum[-1], b_cum[-1]
    sublane = lax.broadcasted_iota(jnp.int32, alpha.shape, 0)
    for k in (1, 2, 4):
        keep = sublane >= k
        alpha_prev = jnp.where(keep, pltpu.roll(alpha, k, axis=0), 1.0)
        beta_prev = jnp.where(keep, pltpu.roll(beta, k, axis=0), 0.0)
        beta = alpha * beta_prev + beta
        alpha = alpha * alpha_prev
    after = alpha * h + beta
    entering = jnp.where(sublane >= 1, pltpu.roll(after, 1, axis=0), h)
    states = [a_cum[r] * entering + b_cum[r] for r in range(SUBLANES)]
    return jnp.stack(states, axis=0), after[SUBLANES - 1:SUBLANES, :]


def _layer_kernel(*refs, final_norm, seq_tiles, prep_next, permute_in, unpermute_out):
    refs = list(refs)
    w_raw_ref = refs.pop(2) if prep_next else None
    w_next_ref = refs.pop(17) if prep_next else None
    (x_ref, x_next_ref, mod_ref, gain_ref, w_in_ref, w_out_ref, conv_a_ref, sgu_w_ref,
     sgu_b_ref, lru_cw_ref, lru_cb_ref, lru_w_ref, lru_ba_ref, lru_bx_ref,
     lam_ref, fgain_ref, o_ref,
     h_scr, h_next_scr, m_scr, p_scr, xc_scr, xcb_scr, vn_scr, pre_scr, z_scr, sgu_wp_scr,
     perm_scr, cx_carry, rx_carry, h_carry) = refs
    (mod_ref, gain_ref, w_in_ref, w_out_ref, conv_a_ref, sgu_w_ref, sgu_b_ref, lru_cw_ref,
     lru_cb_ref, lru_w_ref, lru_ba_ref, lru_bx_ref, lam_ref) = (
        r.at[0] for r in (mod_ref, gain_ref, w_in_ref, w_out_ref, conv_a_ref, sgu_w_ref,
                          sgu_b_ref, lru_cw_ref, lru_cb_ref, lru_w_ref, lru_ba_ref, lru_bx_ref,
                          lam_ref))
    tile = x_ref.shape[1]
    rows = ROW_CHUNK
    n_row_chunks = tile // rows
    n_sgu_chunks = tile // SGU_CHUNK
    step = pl.program_id(0)
    dyn_zero = step // pl.num_programs(0)
    b = step // seq_tiles
    b_next = jnp.minimum(step + 1, pl.num_programs(0) - 1) // seq_tiles

    @pl.when(step % seq_tiles == 0)
    def _():
        cx_carry[...] = jnp.zeros_like(cx_carry)
        rx_carry[...] = jnp.zeros_like(rx_carry)
        h_carry[...] = jnp.zeros_like(h_carry)

    def residual_chunk(src_ref, c):
        xs = src_ref[0, c * rows:(c + 1) * rows, :]
        return _permute_chunk(perm_scr, c, xs, dyn_zero) if permute_in else xs

    def modulated_norm(src_ref, batch, dst_ref):
        shift = mod_ref[0, pl.ds(batch, 1), :]
        scale = mod_ref[1, pl.ds(batch, 1), :]
        gain_scale = gain_ref[...] * (1.0 + scale)
        for c in range(n_row_chunks):
            r0 = c * rows
            xs = residual_chunk(src_ref, c)
            inv = lax.rsqrt(jnp.mean(xs * xs, axis=-1, keepdims=True) + EPS)
            dst_ref[r0:r0 + rows, :] = ((xs * inv) * gain_scale + shift).astype(jnp.bfloat16)

    def project(j, operand_ref, half):
        cols = slice(half * (BLOCK_COLS // 2), (half + 1) * (BLOCK_COLS // 2))
        p_scr[j % P_SLOTS, :, cols] = jnp.dot(operand_ref[...], w_in_ref[j, :, cols],
                                              preferred_element_type=jnp.float32)

    def position_of(row):
        return (row & -ROW_CHUNK) | ((row & (SUBLANES - 1)) * SUBLANES) | (
            (row // SUBLANES) & (SUBLANES - 1))

    @pl.when(step == 0)
    def _():
        modulated_norm(x_ref, b, h_scr)
        project(0, h_scr, 0)
        project(0, h_scr, 1)
        row = lax.broadcasted_iota(jnp.int32, (SGU_CHUNK, SGU_CHUNK), 0)
        col = lax.broadcasted_iota(jnp.int32, (SGU_CHUNK, SGU_CHUNK), 1)
        select_rows = (col == position_of(row)).astype(jnp.bfloat16)
        select_cols = (row == position_of(col)).astype(jnp.bfloat16)
        for head in range(SGU_HEADS):
            w = jnp.where(row >= col, sgu_w_ref[head], 0.0).astype(jnp.bfloat16)
            w = jnp.dot(w, select_cols, preferred_element_type=jnp.float32).astype(jnp.bfloat16)
            sgu_wp_scr[head] = jnp.dot(select_rows, w, preferred_element_type=jnp.float32
                                       ).astype(jnp.bfloat16)

    modulated_norm(x_next_ref, b_next, h_next_scr)
    gate = mod_ref[2, pl.ds(b, 1), :]
    if prep_next:
        for jb in range(N_COL_BLOCKS):
            for k in range(N_SLICES):
                c0 = k * D_MODEL + jb * LANES
                w_next_ref[jb, :, k * LANES:(k + 1) * LANES] = (
                    w_raw_ref[0, :, c0:c0 + LANES].astype(jnp.bfloat16))

    def mix(j):
        col = j * LANES
        lanes = slice(col, col + LANES)
        p = p_scr.at[j % P_SLOTS]
        xc_s, xcb_s, vn_s, pre_s, z_s = (
            r.at[j % 2] for r in (xc_scr, xcb_scr, vn_scr, pre_scr, z_scr))

        def sl(k, r0):
            return p[r0:r0 + rows, k * LANES:(k + 1) * LANES]

        def sl3(k, r0):
            return sl(k, r0).reshape(SUBLANES, SUBLANES, LANES)

        rx_tail = rx_carry[:, :, lanes]
        for c in range(n_row_chunks):
            r0 = c * rows
            xcp, rx_tail = _causal_conv_permuted(sl3(R_X, r0), rx_tail, lru_cw_ref, col)
            xc = (xcp + lru_cb_ref[:, lanes]).reshape(rows, LANES)
            xc_s[r0:r0 + rows, :] = xc
            xcb_s[r0:r0 + rows, :] = xc.astype(jnp.bfloat16)
            v = sl(S_V, r0)
            vc = v - jnp.mean(v, axis=-1, keepdims=True)
            var = jnp.mean(vc * vc, axis=-1, keepdims=True)
            vn_s[r0:r0 + rows, :] = (vc * lax.rsqrt(var + EPS)).astype(jnp.bfloat16)
        rx_carry[:, :, lanes] = rx_tail

        pre_s[...] = jnp.dot(xcb_s[...], lru_w_ref[j], preferred_element_type=jnp.float32)
        vn_wide = jnp.concatenate(
            [vn_s[n * SGU_CHUNK:(n + 1) * SGU_CHUNK, :] for n in range(n_sgu_chunks)], axis=1)
        z_s[...] = jnp.dot(sgu_wp_scr[j], vn_wide, preferred_element_type=jnp.float32)
        yield

        lam = lam_ref[:, lanes]
        softplus_neg_lam = jnp.maximum(-lam, 0.0) + jnp.log1p(jnp.exp(-jnp.abs(lam)))
        log2_a_per_r = (-LRU_C * LOG2_E) * softplus_neg_lam
        cx_tail = cx_carry[:, :, lanes]
        state = h_carry[:, lanes]
        for c in range(n_row_chunks):
            r0 = c * rows
            chunk, q0 = divmod(r0, SGU_CHUNK)

            convp, cx_tail = _causal_conv_permuted(sl3(A_C, r0) * sl3(A_X, r0), cx_tail,
                                                   conv_a_ref, col)
            conv = convp.reshape(rows, LANES)
            merged = _sigmoid(sl(G_A, r0)) * ((_silu(sl(A_Z, r0)) * sl(A_B, r0)) * conv)

            z = (z_s[q0:q0 + rows, chunk * LANES:(chunk + 1) * LANES]
                 + sgu_b_ref[j, q0:q0 + rows, :])
            merged = merged + _sigmoid(sl(G_S, r0)) * ((_silu(sl(S_Z, r0)) * sl(S_U, r0)) * z)

            r =_sigmoid(pre_s[r0:r0 + rows, :LANES] + lru_ba_ref[:, lanes])
            i = _sigmoid(pre_s[r0:r0 + rows, LANES:] + lru_bx_ref[:, lanes])
            a = jnp.exp2(r * log2_a_per_r)
            y = 1.0 - a * a
            bb = (y * lax.rsqrt(jnp.maximum(y, 1e-30))) * (i * xc_s[r0:r0 + rows, :])
            hsp, state = _linear_scan_permuted(a.reshape(SUBLANES, SUBLANES, LANES),
                                               bb.reshape(SUBLANES, SUBLANES, LANES), state)
            merged = merged + _sigmoid(sl(G_R, r0)) * (
                _silu(sl(R_Z, r0)) * hsp.reshape(rows, LANES))

            m_scr[r0:r0 + rows, lanes] = merged.astype(jnp.bfloat16)
        cx_carry[:, :, lanes] = cx_tail
        h_carry[:, lanes] = state

    for j in range(N_COL_BLOCKS):
        upcoming = (j + 1, h_scr) if j + 1 < N_COL_BLOCKS else (0, h_next_scr)
        mix_parts = mix(j)
        project(*upcoming, 0)
        next(mix_parts)
        project(*upcoming, 1)
        next(mix_parts, None)

    res = p_scr.at[(N_COL_BLOCKS - 1) % P_SLOTS]
    res[:, :D_MODEL] = jnp.dot(m_scr[...], w_out_ref[...], preferred_element_type=jnp.float32)
    for c in range(n_row_chunks):
        r0 = c * rows
        out = residual_chunk(x_ref, c) + gate * res[r0:r0 + rows, :D_MODEL]
        if final_norm:
            inv_o = lax.rsqrt(jnp.mean(out * out, axis=-1, keepdims=True) + EPS)
            out = (out * inv_o) * fgain_ref[...]
        if unpermute_out:
            out = _permute_chunk(perm_scr, c, out, dyn_zero)
        o_ref[0, r0:r0 + rows, :] = out
    h_scr[...] = h_next_scr[...]


def _layer_resident(arr, layer):
    index = (layer,) + (0,) * (arr.ndim - 1)
    return pl.BlockSpec((1,) + arr.shape[1:], lambda i: index, pipeline_mode=pl.Buffered(1))


def _tile_index(i, *, seq_tiles, last, ahead):
    t = jnp.minimum(i + ahead, last)
    return (t // seq_tiles, t % seq_tiles, 0)


def _layer(x, layer, stacked, w_in_blocked, w_in_raw, final_gain, *, last):
    bsz, seq, d = x.shape
    tile = SEQ_TILE
    seq_tiles = seq // tile
    n_steps = bsz * seq_tiles
    prep_next = layer + 1 < w_in_raw.shape[0]
    slab = d // n_steps
    tile_spec = functools.partial(_tile_index, seq_tiles=seq_tiles, last=n_steps - 1)
    operands =[(arr, layer) for arr in stacked]
    operands.insert(2, (w_in_blocked[None], 0))
    in_specs = [pl.BlockSpec((1, tile, d), functools.partial(tile_spec, ahead=0)),
                pl.BlockSpec((1, tile, d), functools.partial(tile_spec, ahead=1))]
    inputs = [x, x]
    out_specs = [pl.BlockSpec((1, tile, d), functools.partial(tile_spec, ahead=0))]
    out_shape = [jax.ShapeDtypeStruct(x.shape, x.dtype)]
    if prep_next:
        in_specs.append(pl.BlockSpec((1, slab, w_in_raw.shape[2]), lambda i: (layer + 1, i, 0)))
        inputs.append(w_in_raw)
        out_specs.append(pl.BlockSpec((N_COL_BLOCKS, slab, BLOCK_COLS), lambda i: (0, i, 0)))
        out_shape.append(jax.ShapeDtypeStruct((N_COL_BLOCKS, d, BLOCK_COLS), jnp.bfloat16))
    in_specs += [_layer_resident(arr, index) for arr, index in operands]
    in_specs.append(pl.BlockSpec(final_gain.shape, lambda i: (0, 0), pipeline_mode=pl.Buffered(1)))
    inputs += [arr for arr, _ in operands] + [final_gain]
    outs = pl.pallas_call(
        functools.partial(_layer_kernel, final_norm=last, seq_tiles=seq_tiles,
                          prep_next=prep_next, permute_in=(layer == 0), unpermute_out=last),
        grid=(n_steps,),
        in_specs=in_specs,
        out_specs=out_specs,
        out_shape=out_shape,
        scratch_shapes=[
            pltpu.VMEM((tile, d), jnp.bfloat16),
            pltpu.VMEM((tile, d), jnp.bfloat16),
            pltpu.VMEM((tile, d), jnp.bfloat16),
            pltpu.VMEM((P_SLOTS, tile, BLOCK_COLS), jnp.float32),
            pltpu.VMEM((2, tile, LANES), jnp.float32),
            pltpu.VMEM((2, tile, LANES), jnp.bfloat16),
            pltpu.VMEM((2, tile, LANES), jnp.bfloat16),
            pltpu.VMEM((2, tile, 2 * LANES), jnp.float32),
            pltpu.VMEM((2, SGU_CHUNK, tile), jnp.float32),
            pltpu.VMEM((SGU_HEADS, SGU_CHUNK, SGU_CHUNK), jnp.bfloat16),
            pltpu.VMEM((2, tile // ROW_CHUNK * PERM_ROWS, LANES), jnp.float32),
            pltpu.VMEM((CONV_K - 1, SUBLANES, d), jnp.float32),
            pltpu.VMEM((LRU_CONV_K - 1, SUBLANES, d), jnp.float32),
            pltpu.VMEM((1, d), jnp.float32),
        ],
        compiler_params=pltpu.CompilerParams(
            dimension_semantics=("arbitrary",),
            vmem_limit_bytes=VMEM_LIMIT_BYTES),
        name="hybrid_mixer_layer",
    )(*inputs)
    return (outs[0], outs[1]) if prep_next else (outs[0], None)


def _block_columns_kernel(*refs):
    o_ref = refs[-1]
    for k in range(N_SLICES):
        o_ref[0, :, k * LANES:(k + 1) * LANES] = refs[k][0].astype(jnp.bfloat16)


def _slice_block_index(j, *, layer, k):
    return (layer, 0, k * N_COL_BLOCKS + j)


def _block_columns(w_in, layer):
    _, d, _ = w_in.shape
    return pl.pallas_call(
        _block_columns_kernel,
        grid=(N_COL_BLOCKS,),
        in_specs=[pl.BlockSpec((1, d, LANES), functools.partial(_slice_block_index, layer=layer, k=k))
                  for k in range(N_SLICES)],
        out_specs=pl.BlockSpec((1, d, BLOCK_COLS), lambda j: (j, 0, 0)),
        out_shape=jax.ShapeDtypeStruct((N_COL_BLOCKS, d, BLOCK_COLS), jnp.bfloat16),
        name="block_columns_bf16",
    )(*([w_in] * N_SLICES))


def _permute_positions(arr, axis):
    axis = axis % arr.ndim
    shape = arr.shape
    split = arr.reshape(shape[:axis] + (SGU_CHUNK // ROW_CHUNK, SUBLANES, SUBLANES) + shape[axis + 1:])
    return jnp.swapaxes(split, axis + 1, axis + 2).reshape(shape)


def _pair_block_diag(w_a, w_x):
    def pairs(w):
        w = w.reshape(w.shape[0], LRU_HEADS // 2, 2, LRU_HEAD_DIM, LRU_HEAD_DIM)
        zero = jnp.zeros_like(w[:, :, 0])
        top = jnp.concatenate([w[:, :, 0], zero], axis=-1)
        bottom = jnp.concatenate([zero, w[:, :, 1]], axis=-1)
        return jnp.concatenate([top, bottom], axis=-2)
    return jnp.concatenate([pairs(w_a), pairs(w_x)], axis=-1)


def kernel(x, c, norm_gain, w_mod, b_mod, w_in, w_out, conv_a_w, sgu_w, sgu_b, lru_conv_w,
           lru_conv_b, lru_wa, lru_ba, lru_wx, lru_bx, lru_lambda, final_gain):
    depth = w_in.shape[0]
    d = x.shape[-1]
    stacked = (
        _modulation(c, w_mod, b_mod),
        norm_gain.reshape(depth, 1, d),
        w_out.astype(jnp.bfloat16),
        conv_a_w,
        sgu_w,
        jnp.broadcast_to(_permute_positions(sgu_b, -1)[..., None], sgu_b.shape + (LANES,)),
        lru_conv_w,
        lru_conv_b.reshape(depth, 1, d),
        _pair_block_diag(lru_wa, lru_wx).astype(jnp.bfloat16),
        lru_ba.reshape(depth, 1, d),
        lru_bx.reshape(depth, 1, d),
        lru_lambda.reshape(depth, 1, d),
    )
    w_in_blocked = _block_columns(w_in, 0)
    for l in range(depth):
        x, w_in_blocked = _layer(x, l, stacked, w_in_blocked, w_in, final_gain.reshape(1, d),
                                 last=(l == depth - 1))
    return x
```

```python
import functools
import math

import jax
import jax.numpy as jnp
from jax import lax
from jax.experimental import pallas as pl
from jax.experimental.pallas import tpu as pltpu

D_MODEL = 1024
SGU_CHUNK = 128
SGU_HEADS = 8
LRU_HEADS = 16
LRU_HEAD_DIM = 64
CONV_K = 3
LRU_CONV_K = 4
LRU_C = 8.0
EPS = 1e-6
LOG2_E = math.log2(math.e)

SUBLANES = 8
LANES = 128
N_SLICES = 12
(R_X, S_V, A_X, A_B, A_C, A_Z, S_U, S_Z, R_Z, G_A, G_S, G_R) = range(N_SLICES)
SLICE_SOURCE = (7, 5, 0, 1, 2, 3, 4, 6, 8, 9, 10, 11)
N_COL_BLOCKS = D_MODEL // LANES
BLOCK_COLS = N_SLICES * LANES
SEQ_TILE = 512
P_SLOTS = 2
ROW_CHUNK = SUBLANES * SUBLANES
PERM_PITCH = SUBLANES + 1
PERM_ROWS = SUBLANES * PERM_PITCH
VMEM_LIMIT_BYTES = 60 * 1024 * 1024


def _sigmoid(v):
    return 1.0 / (1.0 + jnp.exp2(v * (-LOG2_E)))


def _silu(v):
    return v * _sigmoid(v)


def _mod_kernel(c_ref, w_ref, b_ref, o_ref):
    c = c_ref[...]
    o_ref[0, 0] = jnp.dot(_silu(c), w_ref[0], preferred_element_type=jnp.float32) + b_ref[0, 0]


def _modulation(c, w_mod, b_mod):
    depth, d, _ = w_mod.shape
    bsz = c.shape[0]
    return pl.pallas_call(
        _mod_kernel,
        grid=(depth, 3),
        in_specs=[
            pl.BlockSpec((bsz, d), lambda l, k: (0, 0)),
            pl.BlockSpec((1, d, d), lambda l, k: (l, 0, k)),
            pl.BlockSpec((1, 1, 1, d), lambda l, k: (l, k, 0, 0)),
        ],
        out_specs=pl.BlockSpec((1, 1, bsz, d), lambda l, k: (l, k, 0, 0)),
        out_shape=jax.ShapeDtypeStruct((depth, 3, bsz, d), jnp.float32),
        name="adaln_modulation",
    )(c, w_mod, b_mod.reshape(depth, 3, 1, d))


def _permute_rows(scr, chunk, v3, dyn_zero):
    base = chunk * PERM_ROWS
    for g in range(SUBLANES):
        scr[pl.ds(base + g, SUBLANES, stride=PERM_PITCH), :] = v3[g]
    return jnp.stack([scr[pl.ds(dyn_zero + (base + PERM_PITCH * r), SUBLANES), :]
                      for r in range(SUBLANES)], axis=0)


def _permute_chunk(scr, chunk, xs, dyn_zero):
    cols = []
    for l in range(xs.shape[1] // LANES):
        v3 = xs[:, l * LANES:(l + 1) * LANES].reshape(SUBLANES, SUBLANES, LANES)
        plane = scr.at[l % scr.shape[0]]
        cols.append(_permute_rows(plane, chunk, v3, dyn_zero).reshape(ROW_CHUNK, LANES))
    return jnp.concatenate(cols, axis=1)


def _causal_conv_permuted(vp3, prev_tail, w_ref, col):
    k_taps = w_ref.shape[0]
    n_prev = k_taps - 1
    tail = vp3[SUBLANES - n_prev:]
    sublane = lax.broadcasted_iota(jnp.int32, tail.shape, 1)
    before = jnp.where(sublane >= 1, pltpu.roll(tail, 1, axis=1), pltpu.roll(prev_tail, 1, axis=1))
    ext = jnp.concatenate([before, vp3], axis=0)
    acc = w_ref[k_taps - 1:k_taps, col:col + LANES] * vp3
    for j in range(n_prev):
        acc = acc + w_ref[j:j + 1, col:col + LANES] * ext[j:j + SUBLANES]
    return acc, tail


def _linear_scan_permuted(ap3, bp3, h):
    a_cum = [ap3[0]]
    b_cum = [bp3[0]]
    for r in range(1, SUBLANES):
        b_cum.append(ap3[r] * b_cum[-1] + bp3[r])
        a_cum.append(ap3[r] * a_cum[-1])
    alpha, beta = a_cum[-1], b_cum[-1]
    sublane = lax.broadcasted_iota(jnp.int32, alpha.shape, 0)
    for k in (1, 2, 4):
        keep = sublane >= k
        alpha_prev = jnp.where(keep, pltpu.roll(alpha, k, axis=0), 1.0)
        beta_prev = jnp.where(keep, pltpu.roll(beta, k, axis=0), 0.0)
        beta = alpha * beta_prev + beta
        alpha = alpha * alpha_prev
    after = alpha * h + beta
    entering = jnp.where(sublane >= 1, pltpu.roll(after, 1, axis=0), h)
    states = [a_cum[r] * entering + b_cum[r] for r in range(SUBLANES)]
    return jnp.stack(states, axis=0), after[SUBLANES - 1:SUBLANES, :]


def _layer_kernel(*refs, final_norm, seq_tiles, prep_next, permute_in, unpermute_out):
    refs = list(refs)
    w_raw_ref = refs.pop(2) if prep_next else None
    w_next_ref = refs.pop(17) if prep_next else None
    (x_ref, x_next_ref, mod_ref, gain_ref, w_in_ref, w_out_ref, conv_a_ref, sgu_w_ref,
     sgu_b_ref, lru_cw_ref, lru_cb_ref, lru_w_ref, lru_ba_ref, lru_bx_ref,
     lam_ref, fgain_ref, o_ref,
     h_scr, h_next_scr, m_scr, p_scr, xc_scr, xcb_scr, vn_scr, pre_scr, z_scr, sgu_wp_scr,
     perm_scr, cx_carry, rx_carry, h_carry) = refs
    (mod_ref, gain_ref, w_in_ref, w_out_ref, conv_a_ref, sgu_w_ref, sgu_b_ref, lru_cw_ref,
     lru_cb_ref, lru_w_ref, lru_ba_ref, lru_bx_ref, lam_ref) = (
        r.at[0] for r in (mod_ref, gain_ref, w_in_ref, w_out_ref, conv_a_ref, sgu_w_ref,
                          sgu_b_ref, lru_cw_ref, lru_cb_ref, lru_w_ref, lru_ba_ref, lru_bx_ref,
                          lam_ref))
    tile = x_ref.shape[1]
    rows = ROW_CHUNK
    n_row_chunks = tile // rows
    n_sgu_chunks = tile // SGU_CHUNK
    step = pl.program_id(0)
    dyn_zero = step // pl.num_programs(0)
    b = step // seq_tiles
    b_next = jnp.minimum(step + 1, pl.num_programs(0) - 1) // seq_tiles

    @pl.when(step % seq_tiles == 0)
    def _():
        cx_carry[...] = jnp.zeros_like(cx_carry)
        h_carry[...] = jnp.zeros_like(h_carry)

    if prep_next:
        for jb in range(N_COL_BLOCKS):
            for k in range(N_SLICES):
                c0 = SLICE_SOURCE[k] * D_MODEL + jb * LANES
                w_next_ref[jb, :, k * LANES:(k + 1) * LANES] = (
                    w_raw_ref[0, :, c0:c0 + LANES].astype(jnp.bfloat16))

    def residual_chunk(src_ref, c):
        xs = src_ref[0, c * rows:(c + 1) * rows, :]
        return _permute_chunk(perm_scr, c, xs, dyn_zero) if permute_in else xs

    def modulated_norm(src_ref, batch, dst_ref):
        shift = mod_ref[0, pl.ds(batch, 1), :]
        scale = mod_ref[1, pl.ds(batch, 1), :]
        gain_scale = gain_ref[...] * (1.0 + scale)
        for c in range(n_row_chunks):
            r0 = c * rows
            xs = residual_chunk(src_ref, c)
            inv = lax.rsqrt(jnp.mean(xs * xs, axis=-1, keepdims=True) + EPS)
            dst_ref[r0:r0 + rows, :] = ((xs * inv) * gain_scale + shift).astype(jnp.bfloat16)

    def project(j, operand_ref, half):
        cols = slice(half * (BLOCK_COLS // 2), (half + 1) * (BLOCK_COLS // 2))
        p_scr[j % P_SLOTS, :, cols] = jnp.dot(operand_ref[...], w_in_ref[j, :, cols],
                                              preferred_element_type=jnp.float32)

    def position_of(row):
        return (row & -ROW_CHUNK) | ((row & (SUBLANES - 1)) * SUBLANES) | (
            (row // SUBLANES) & (SUBLANES - 1))

    def views(j):
        col = j * LANES
        p = p_scr.at[j % P_SLOTS]

        def sl(k, r0):
            return p[r0:r0 + rows, k * LANES:(k + 1) * LANES]

        def sl3(k, r0):
            return sl(k, r0).reshape(SUBLANES, SUBLANES, LANES)

        scratch = tuple(r.at[j % 2] for r in (xc_scr, xcb_scr, vn_scr, pre_scr, z_scr))
        return col, slice(col, col + LANES), sl, sl3, scratch

    def mix_early(j, batch_start):
        col, lanes, sl, sl3, (xc_s, xcb_s, vn_s, _, _) = views(j)
        rx_tail = jnp.where(batch_start, 0.0, rx_carry[:, :, lanes])
        for c in range(n_row_chunks):
            r0 = c * rows
            xcp, rx_tail = _causal_conv_permuted(sl3(R_X, r0), rx_tail, lru_cw_ref, col)
            xc = (xcp + lru_cb_ref[:, lanes]).reshape(rows, LANES)
            xc_s[r0:r0 + rows, :] = xc
            xcb_s[r0:r0 + rows, :] = xc.astype(jnp.bfloat16)
            v = sl(S_V, r0)
            vc = v - jnp.mean(v, axis=-1, keepdims=True)
            var = jnp.mean(vc * vc, axis=-1, keepdims=True)
            vn_s[r0:r0 + rows, :] = (vc * lax.rsqrt(var + EPS)).astype(jnp.bfloat16)
        rx_carry[:, :, lanes] = rx_tail

    def small_matmuls(j):
        _, _, _, _, (_, xcb_s, vn_s, pre_s, z_s) = views(j)
        pre_s[...] = jnp.dot(xcb_s[...], lru_w_ref[j], preferred_element_type=jnp.float32)
        vn_wide = jnp.concatenate(
            [vn_s[n * SGU_CHUNK:(n + 1) * SGU_CHUNK, :] for n in range(n_sgu_chunks)], axis=1)
        z_s[...] = jnp.dot(sgu_wp_scr[j], vn_wide, preferred_element_type=jnp.float32)

    def mix_late(j):
        col, lanes, sl, sl3, (xc_s, _, _, pre_s, z_s) = views(j)
        lam = lam_ref[:, lanes]
        softplus_neg_lam = jnp.maximum(-lam, 0.0) + jnp.log1p(jnp.exp(-jnp.abs(lam)))
        log2_a_per_r = (-LRU_C * LOG2_E) * softplus_neg_lam
        cx_tail = cx_carry[:, :, lanes]
        state = h_carry[:, lanes]
        for c in range(n_row_chunks):
            r0 = c * rows
            chunk, q0 = divmod(r0, SGU_CHUNK)

            convp, cx_tail = _causal_conv_permuted(sl3(A_C, r0) * sl3(A_X, r0), cx_tail,
                                                   conv_a_ref, col)
            conv = convp.reshape(rows, LANES)
            merged = _sigmoid(sl(G_A, r0)) * ((_silu(sl(A_Z, r0)) * sl(A_B, r0)) * conv)

            z = (z_s[q0:q0 + rows, chunk * LANES:(chunk + 1) * LANES]
                 + sgu_b_ref[j, q0:q0 + rows, :])
            merged = merged + _sigmoid(sl(G_S, r0)) * ((_silu(sl(S_Z, r0)) * sl(S_U, r0)) * z)

            r = _sigmoid(pre_s[r0:r0 + rows, :LANES] + lru_ba_ref[:, lanes])
            i = _sigmoid(pre_s[r0:r0 + rows, LANES:] + lru_bx_ref[:, lanes])
            a = jnp.exp2(r * log2_a_per_r)
            y = 1.0 - a * a
            bb = (y * lax.rsqrt(jnp.maximum(y, 1e-30))) * (i * xc_s[r0:r0 + rows, :])
            hsp, state = _linear_scan_permuted(a.reshape(SUBLANES, SUBLANES, LANES),
                                               bb.reshape(SUBLANES, SUBLANES, LANES), state)
            merged = merged + _sigmoid(sl(G_R, r0)) * (
                _silu(sl(R_Z, r0)) * hsp.reshape(rows, LANES))

            m_scr[r0:r0 + rows, lanes] = merged.astype(jnp.bfloat16)
        cx_carry[:, :, lanes] = cx_tail
        h_carry[:, lanes] = state

    @pl.when(step == 0)
    def _():
        rx_carry[...] = jnp.zeros_like(rx_carry)
        modulated_norm(x_ref, b, h_scr)
        project(0, h_scr, 0)
        project(0, h_scr, 1)
        mix_early(0, True)
        row = lax.broadcasted_iota(jnp.int32, (SGU_CHUNK, SGU_CHUNK), 0)
        col = lax.broadcasted_iota(jnp.int32, (SGU_CHUNK, SGU_CHUNK), 1)
        select_rows = (col == position_of(row)).astype(jnp.bfloat16)
        select_cols = (row == position_of(col)).astype(jnp.bfloat16)
        for head in range(SGU_HEADS):
            w = jnp.where(row >= col, sgu_w_ref[head], 0.0).astype(jnp.bfloat16)
            w = jnp.dot(w, select_cols, preferred_element_type=jnp.float32).astype(jnp.bfloat16)
            sgu_wp_scr[head] = jnp.dot(select_rows, w, preferred_element_type=jnp.float32
                                       ).astype(jnp.bfloat16)

    modulated_norm(x_next_ref, b_next, h_next_scr)
    gate = mod_ref[2, pl.ds(b, 1), :]

    this_starts_batch = step % seq_tiles == 0
    next_starts_batch = (step + 1) % seq_tiles == 0
    for j in range(N_COL_BLOCKS):
        last_block = j + 1 == N_COL_BLOCKS
        upcoming = (0, h_next_scr) if last_block else (j + 1, h_scr)
        small_matmuls(j)
        project(*upcoming, 0)
        mix_late(j)
        project(*upcoming, 1)
        mix_early(upcoming[0], next_starts_batch if last_block else this_starts_batch)

    res = p_scr.at[(N_COL_BLOCKS - 1) % P_SLOTS]
    res[:, :D_MODEL] = jnp.dot(m_scr[...], w_out_ref[...], preferred_element_type=jnp.float32)
    for c in range(n_row_chunks):
        r0 = c * rows
        out = residual_chunk(x_ref, c) + gate * res[r0:r0 + rows, :D_MODEL]
        if final_norm:
            inv_o = lax.rsqrt(jnp.mean(out * out, axis=-1, keepdims=True) + EPS)
            out = (out * inv_o) * fgain_ref[...]
        if unpermute_out:
            out = _permute_chunk(perm_scr, c, out, dyn_zero)
        o_ref[0, r0:r0 + rows, :] = out
    h_scr[...] = h_next_scr[...]


def _layer_resident(arr, layer):
    index = (layer,) + (0,) * (arr.ndim - 1)
    return pl.BlockSpec((1,) + arr.shape[1:], lambda i: index, pipeline_mode=pl.Buffered(1))


def _tile_index(i, *, seq_tiles, last, ahead):
    t = jnp.minimum(i + ahead, last)
    return (t // seq_tiles, t % seq_tiles, 0)


def _layer(x, layer, stacked, w_in_blocked, w_in_raw, final_gain, *, last):
    bsz, seq, d = x.shape
    tile = SEQ_TILE
    seq_tiles = seq // tile
    n_steps = bsz * seq_tiles
    prep_next = layer + 1 < w_in_raw.shape[0]
    slab = d // n_steps
    tile_spec = functools.partial(_tile_index, seq_tiles=seq_tiles, last=n_steps - 1)
    operands =[(arr, layer) for arr in stacked]
    operands.insert(2, (w_in_blocked[None], 0))
    in_specs = [pl.BlockSpec((1, tile, d), functools.partial(tile_spec, ahead=0)),
                pl.BlockSpec((1, tile, d), functools.partial(tile_spec, ahead=1))]
    inputs = [x, x]
    out_specs = [pl.BlockSpec((1, tile, d), functools.partial(tile_spec, ahead=0))]
    out_shape = [jax.ShapeDtypeStruct(x.shape, x.dtype)]
    if prep_next:
        in_specs.append(pl.BlockSpec((1, slab, w_in_raw.shape[2]), lambda i: (layer + 1, i, 0)))
        inputs.append(w_in_raw)
        out_specs.append(pl.BlockSpec((N_COL_BLOCKS, slab, BLOCK_COLS), lambda i: (0, i, 0)))
        out_shape.append(jax.ShapeDtypeStruct((N_COL_BLOCKS, d, BLOCK_COLS), jnp.bfloat16))
    in_specs += [_layer_resident(arr, index) for arr, index in operands]
    in_specs.append(pl.BlockSpec(final_gain.shape, lambda i: (0, 0), pipeline_mode=pl.Buffered(1)))
    inputs += [arr for arr, _ in operands] + [final_gain]
    outs = pl.pallas_call(
        functools.partial(_layer_kernel, final_norm=last, seq_tiles=seq_tiles,
                          prep_next=prep_next, permute_in=(layer == 0), unpermute_out=last),
        grid=(n_steps,),
        in_specs=in_specs,
        out_specs=out_specs,
        out_shape=out_shape,
        scratch_shapes=[
            pltpu.VMEM((tile, d), jnp.bfloat16),
            pltpu.VMEM((tile, d), jnp.bfloat16),
            pltpu.VMEM((tile, d), jnp.bfloat16),
            pltpu.VMEM((P_SLOTS, tile, BLOCK_COLS), jnp.float32),
            pltpu.VMEM((2, tile, LANES), jnp.float32),
            pltpu.VMEM((2, tile, LANES), jnp.bfloat16),
            pltpu.VMEM((2, tile, LANES), jnp.bfloat16),
            pltpu.VMEM((2, tile, 2 * LANES), jnp.float32),
            pltpu.VMEM((2, SGU_CHUNK, tile), jnp.float32),
            pltpu.VMEM((SGU_HEADS, SGU_CHUNK, SGU_CHUNK), jnp.bfloat16),
            pltpu.VMEM((2, tile // ROW_CHUNK * PERM_ROWS, LANES), jnp.float32),
            pltpu.VMEM((CONV_K - 1, SUBLANES, d), jnp.float32),
            pltpu.VMEM((LRU_CONV_K - 1, SUBLANES, d), jnp.float32),
            pltpu.VMEM((1, d), jnp.float32),
        ],
        compiler_params=pltpu.CompilerParams(
            dimension_semantics=("arbitrary",),
            vmem_limit_bytes=VMEM_LIMIT_BYTES),
        name="hybrid_mixer_layer",
    )(*inputs)
    return (outs[0], outs[1]) if prep_next else (outs[0], None)


def _block_columns_kernel(*refs):
    o_ref = refs[-1]
    for k in range(N_SLICES):
        o_ref[0, :, k * LANES:(k + 1) * LANES] = refs[k][0].astype(jnp.bfloat16)


def _slice_block_index(j, *, layer, k):
    return (layer, 0, SLICE_SOURCE[k] * N_COL_BLOCKS + j)


def _block_columns(w_in, layer):
    _, d, _ = w_in.shape
    return pl.pallas_call(
        _block_columns_kernel,
        grid=(N_COL_BLOCKS,),
        in_specs=[pl.BlockSpec((1, d, LANES), functools.partial(_slice_block_index, layer=layer, k=k))
                  for k in range(N_SLICES)],
        out_specs=pl.BlockSpec((1, d, BLOCK_COLS), lambda j: (j, 0, 0)),
        out_shape=jax.ShapeDtypeStruct((N_COL_BLOCKS, d, BLOCK_COLS), jnp.bfloat16),
        name="block_columns_bf16",
    )(*([w_in] * N_SLICES))


def _permute_positions(arr, axis):
    axis = axis % arr.ndim
    shape = arr.shape
    split = arr.reshape(shape[:axis] + (SGU_CHUNK // ROW_CHUNK, SUBLANES, SUBLANES) + shape[axis + 1:])
    return jnp.swapaxes(split, axis + 1, axis + 2).reshape(shape)


def _pair_block_diag(w_a, w_x):
    def pairs(w):
        w = w.reshape(w.shape[0], LRU_HEADS // 2, 2, LRU_HEAD_DIM, LRU_HEAD_DIM)
        zero = jnp.zeros_like(w[:, :, 0])
        top = jnp.concatenate([w[:, :, 0], zero], axis=-1)
        bottom = jnp.concatenate([zero, w[:, :, 1]], axis=-1)
        return jnp.concatenate([top, bottom], axis=-2)
    return jnp.concatenate([pairs(w_a), pairs(w_x)], axis=-1)


def kernel(x, c, norm_gain, w_mod, b_mod, w_in, w_out, conv_a_w, sgu_w, sgu_b, lru_conv_w,
           lru_conv_b, lru_wa, lru_ba, lru_wx, lru_bx, lru_lambda, final_gain):
    depth = w_in.shape[0]
    d = x.shape[-1]
    stacked = (
        _modulation(c, w_mod, b_mod),
        norm_gain.reshape(depth, 1, d),
        w_out.astype(jnp.bfloat16),
        conv_a_w,
        sgu_w,
        jnp.broadcast_to(_permute_positions(sgu_b, -1)[..., None], sgu_b.shape + (LANES,)),
        lru_conv_w,
        lru_conv_b.reshape(depth, 1, d),
        _pair_block_diag(lru_wa, lru_wx).astype(jnp.bfloat16),
        lru_ba.reshape(depth, 1, d),
        lru_bx.reshape(depth, 1, d),
        lru_lambda.reshape(depth, 1, d),
    )
    w_in_blocked = _block_columns(w_in, 0)
    for l in range(depth):
        x, w_in_blocked = _layer(x, l, stacked, w_in_blocked, w_in, final_gain.reshape(1, d),
                                 last=(l == depth - 1))
    return x
```

```python
import functools
import math

import jax
import jax.numpy as jnp
from jax import lax
from jax.experimental import pallas as pl
from jax.experimental.pallas import tpu as pltpu

D_MODEL = 1024
SGU_CHUNK = 128
SGU_HEADS = 8
LRU_HEADS = 16
LRU_HEAD_DIM = 64
CONV_K = 3
LRU_CONV_K = 4
LRU_C = 8.0
EPS = 1e-6
LOG2_E = math.log2(math.e)

SUBLANES = 8
LANES = 128
N_SLICES = 12
(A_X, A_B, A_C, A_Z, S_U, S_V, S_Z, R_X, R_Z, G_A, G_S, G_R) = range(N_SLICES)
N_COL_BLOCKS = D_MODEL // LANES
BLOCK_COLS = N_SLICES * LANES
SEQ_TILE = 512
P_SLOTS = 2
ROW_CHUNK = SUBLANES * SUBLANES
PERM_PITCH = SUBLANES + 1
PERM_ROWS = SUBLANES * PERM_PITCH
VMEM_LIMIT_BYTES = 60 * 1024 * 1024


def _sigmoid(v):
    return 1.0 / (1.0 + jnp.exp2(v * (-LOG2_E)))


def _silu(v):
    return v * _sigmoid(v)


def _mod_kernel(c_ref, w_ref, b_ref, o_ref):
    c = c_ref[...]
    o_ref[0, 0] = jnp.dot(_silu(c), w_ref[0], preferred_element_type=jnp.float32) + b_ref[0, 0]


def _modulation(c, w_mod, b_mod):
    depth, d, _ = w_mod.shape
    bsz = c.shape[0]
    return pl.pallas_call(
        _mod_kernel,
        grid=(depth, 3),
        in_specs=[
            pl.BlockSpec((bsz, d), lambda l, k: (0, 0)),
            pl.BlockSpec((1, d, d), lambda l, k: (l, 0, k)),
            pl.BlockSpec((1, 1, 1, d), lambda l, k: (l, k, 0, 0)),
        ],
        out_specs=pl.BlockSpec((1, 1, bsz, d), lambda l, k: (l, k, 0, 0)),
        out_shape=jax.ShapeDtypeStruct((depth, 3, bsz, d), jnp.float32),
        name="adaln_modulation",
    )(c, w_mod, b_mod.reshape(depth, 3, 1, d))


def _permute_rows(scr, chunk, v3, dyn_zero):
    base = chunk * PERM_ROWS
    for g in range(SUBLANES):
        scr[pl.ds(base + g, SUBLANES, stride=PERM_PITCH), :] = v3[g]
    return jnp.stack([scr[pl.ds(dyn_zero + (base + PERM_PITCH * r), SUBLANES), :]
                      for r in range(SUBLANES)], axis=0)


def _permute_chunk(scr, chunk, xs, dyn_zero):
    cols = []
    for l in range(xs.shape[1] // LANES):
        v3 = xs[:, l * LANES:(l + 1) * LANES].reshape(SUBLANES, SUBLANES, LANES)
        plane = scr.at[l % scr.shape[0]]
        cols.append(_permute_rows(plane, chunk, v3, dyn_zero).reshape(ROW_CHUNK, LANES))
    return jnp.concatenate(cols, axis=1)


def _causal_conv_permuted(vp3, prev_tail, w_ref, col):
    k_taps = w_ref.shape[0]
    n_prev = k_taps - 1
    tail = vp3[SUBLANES - n_prev:]
    sublane = lax.broadcasted_iota(jnp.int32, tail.shape, 1)
    before = jnp.where(sublane >= 1, pltpu.roll(tail, 1, axis=1), pltpu.roll(prev_tail, 1, axis=1))
    ext = jnp.concatenate([before, vp3], axis=0)
    acc = w_ref[k_taps - 1:k_taps, col:col + LANES] * vp3
    for j in range(n_prev):
        acc = acc + w_ref[j:j + 1, col:col + LANES] * ext[j:j + SUBLANES]
    return acc, tail


def _linear_scan_permuted(ap3, bp3, h):
    a_cum = [ap3[0]]
    b_cum = [bp3[0]]
    for r in range(1, SUBLANES):
        b_cum.append(ap3[r] * b_cum[-1] + bp3[r])
        a_cum.append(ap3[r] * a_cum[-1])
    alpha, beta = a_cum[-1], b_cum[-1]
    sublane = lax.broadcasted_iota(jnp.int32, alpha.shape, 0)
    for k in (1, 2, 4):
        keep = sublane >= k
        alpha_prev = jnp.where(keep, pltpu.roll(alpha, k, axis=0), 1.0)
        beta_prev = jnp.where(keep, pltpu.roll(beta, k, axis=0), 0.0)
        beta = alpha * beta_prev + beta
        alpha = alpha * alpha_prev
    after = alpha * h + beta
    entering = jnp.where(sublane >= 1, pltpu.roll(after, 1, axis=0), h)
    states = [a_cum[r] * entering + b_cum[r] for r in range(SUBLANES)]
    return jnp.stack(states, axis=0), after[SUBLANES - 1:SUBLANES, :]


def _layer_kernel(*refs, final_norm, seq_tiles, prep_next, permute_in, unpermute_out):
    refs = list(refs)
    w_raw_ref = refs.pop(2) if prep_next else None
    w_next_ref = refs.pop(17) if prep_next else None
    (x_ref, x_next_ref, mod_ref, gain_ref, w_in_ref, w_out_ref, conv_a_ref, sgu_w_ref,
     sgu_b_ref, lru_cw_ref, lru_cb_ref, lru_w_ref, lru_ba_ref, lru_bx_ref,
     lam_ref, fgain_ref, o_ref,
     h_scr, h_next_scr, m_scr, p_scr, xc_scr, xcb_scr, vn_scr, pre_scr, z_scr, sgu_wp_scr,
     perm_scr, cx_carry, rx_carry, h_carry) = refs
    (mod_ref, gain_ref, w_in_ref, w_out_ref, conv_a_ref, sgu_w_ref, sgu_b_ref, lru_cw_ref,
     lru_cb_ref, lru_w_ref, lru_ba_ref, lru_bx_ref, lam_ref) = (
        r.at[0] for r in (mod_ref, gain_ref, w_in_ref, w_out_ref, conv_a_ref, sgu_w_ref,
                          sgu_b_ref, lru_cw_ref, lru_cb_ref, lru_w_ref, lru_ba_ref, lru_bx_ref,
                          lam_ref))
    tile = x_ref.shape[1]
    rows = ROW_CHUNK
    n_row_chunks = tile // rows
    n_sgu_chunks = tile // SGU_CHUNK
    step = pl.program_id(0)
    dyn_zero = step // pl.num_programs(0)
    b = step // seq_tiles
    b_next = jnp.minimum(step + 1, pl.num_programs(0) - 1) // seq_tiles

    @pl.when(step % seq_tiles == 0)
    def _():
        cx_carry[...] = jnp.zeros_like(cx_carry)
        rx_carry[...] = jnp.zeros_like(rx_carry)
        h_carry[...] = jnp.zeros_like(h_carry)

    if prep_next:
        for jb in range(N_COL_BLOCKS):
            for k in range(N_SLICES):
                c0 = k * D_MODEL + jb * LANES
                w_next_ref[jb, :, k * LANES:(k + 1) * LANES] = (
                    w_raw_ref[0, :, c0:c0 + LANES].astype(jnp.bfloat16))

    def residual_chunk(src_ref, c):
        xs = src_ref[0, c * rows:(c + 1) * rows, :]
        return _permute_chunk(perm_scr, c, xs, dyn_zero) if permute_in else xs

    def modulated_norm(src_ref, batch, dst_ref):
        shift = mod_ref[0, pl.ds(batch, 1), :]
        scale = mod_ref[1, pl.ds(batch, 1), :]
        gain_scale = gain_ref[...] * (1.0 + scale)
        for c in range(n_row_chunks):
            r0 = c * rows
            xs = residual_chunk(src_ref, c)
            inv = lax.rsqrt(jnp.mean(xs * xs, axis=-1, keepdims=True) + EPS)
            dst_ref[r0:r0 + rows, :] = ((xs * inv) * gain_scale + shift).astype(jnp.bfloat16)

    def project(j, operand_ref, half):
        cols = slice(0, BLOCK_COLS // 3) if half == 0 else slice(BLOCK_COLS // 3, BLOCK_COLS)
        p_scr[j % P_SLOTS, :, cols] = jnp.dot(operand_ref[...], w_in_ref[j, :, cols],
                                              preferred_element_type=jnp.float32)

    def position_of(row):
        return (row & -ROW_CHUNK) | ((row & (SUBLANES - 1)) * SUBLANES) | (
            (row // SUBLANES) & (SUBLANES - 1))

    @pl.when(step == 0)
    def _():
        modulated_norm(x_ref, b, h_scr)
        project(0, h_scr, 0)
        project(0, h_scr, 1)
        row = lax.broadcasted_iota(jnp.int32, (SGU_CHUNK, SGU_CHUNK), 0)
        col = lax.broadcasted_iota(jnp.int32, (SGU_CHUNK, SGU_CHUNK), 1)
        select_rows = (col == position_of(row)).astype(jnp.bfloat16)
        select_cols = (row == position_of(col)).astype(jnp.bfloat16)
        for head in range(SGU_HEADS):
            w = jnp.where(row >= col, sgu_w_ref[head], 0.0).astype(jnp.bfloat16)
            w = jnp.dot(w, select_cols, preferred_element_type=jnp.float32).astype(jnp.bfloat16)
            sgu_wp_scr[head] = jnp.dot(select_rows, w, preferred_element_type=jnp.float32
                                       ).astype(jnp.bfloat16)

    modulated_norm(x_next_ref, b_next, h_next_scr)
    gate = mod_ref[2, pl.ds(b, 1), :]

    def mix(j):
        col = j * LANES
        lanes = slice(col, col + LANES)
        p = p_scr.at[j % P_SLOTS]
        xc_s, xcb_s, vn_s, pre_s, z_s = (
            r.at[j % 2] for r in (xc_scr, xcb_scr, vn_scr, pre_scr, z_scr))

        def sl(k, r0):
            return p[r0:r0 + rows, k * LANES:(k + 1) * LANES]

        def sl3(k, r0):
            return sl(k, r0).reshape(SUBLANES, SUBLANES, LANES)

        rx_tail = rx_carry[:, :, lanes]
        for c in range(n_row_chunks):
            r0 = c * rows
            xcp, rx_tail = _causal_conv_permuted(sl3(R_X, r0), rx_tail, lru_cw_ref, col)
            xc = (xcp + lru_cb_ref[:, lanes]).reshape(rows, LANES)
            xc_s[r0:r0 + rows, :] = xc
            xcb_s[r0:r0 + rows, :] = xc.astype(jnp.bfloat16)
            v = sl(S_V, r0)
            vc = v - jnp.mean(v, axis=-1, keepdims=True)
            var = jnp.mean(vc * vc, axis=-1, keepdims=True)
            vn_s[r0:r0 + rows, :] = (vc * lax.rsqrt(var + EPS)).astype(jnp.bfloat16)
        rx_carry[:, :, lanes] = rx_tail

        pre_s[...] = jnp.dot(xcb_s[...], lru_w_ref[j], preferred_element_type=jnp.float32)
        vn_wide = jnp.concatenate(
            [vn_s[n * SGU_CHUNK:(n + 1) * SGU_CHUNK, :] for n in range(n_sgu_chunks)], axis=1)
        z_s[...] = jnp.dot(sgu_wp_scr[j], vn_wide, preferred_element_type=jnp.float32)
        yield

        lam = lam_ref[:, lanes]
        softplus_neg_lam = jnp.maximum(-lam, 0.0) + jnp.log1p(jnp.exp(-jnp.abs(lam)))
        log2_a_per_r = (-LRU_C * LOG2_E) * softplus_neg_lam
        cx_tail = cx_carry[:, :, lanes]
        state = h_carry[:, lanes]
        for c in range(n_row_chunks):
            r0 = c * rows
            chunk, q0 = divmod(r0, SGU_CHUNK)

            convp, cx_tail = _causal_conv_permuted(sl3(A_C, r0) * sl3(A_X, r0), cx_tail,
                                                   conv_a_ref, col)
            conv = convp.reshape(rows, LANES)
            merged = _sigmoid(sl(G_A, r0)) * ((_silu(sl(A_Z, r0)) * sl(A_B, r0)) * conv)

            z = (z_s[q0:q0 + rows, chunk * LANES:(chunk + 1) * LANES]
                 + sgu_b_ref[j, q0:q0 + rows, :])
            merged = merged + _sigmoid(sl(G_S, r0)) * ((_silu(sl(S_Z, r0)) * sl(S_U, r0)) * z)

            r =_sigmoid(pre_s[r0:r0 + rows, :LANES] + lru_ba_ref[:, lanes])
            i = _sigmoid(pre_s[r0:r0 + rows, LANES:] + lru_bx_ref[:, lanes])
            a = jnp.exp2(r * log2_a_per_r)
            y = 1.0 - a * a
            bb = (y * lax.rsqrt(jnp.maximum(y, 1e-30))) * (i * xc_s[r0:r0 + rows, :])
            hsp, state = _linear_scan_permuted(a.reshape(SUBLANES, SUBLANES, LANES),
                                               bb.reshape(SUBLANES, SUBLANES, LANES), state)
            merged = merged + _sigmoid(sl(G_R, r0)) * (
                _silu(sl(R_Z, r0)) * hsp.reshape(rows, LANES))

            m_scr[r0:r0 + rows, lanes] = merged.astype(jnp.bfloat16)
        cx_carry[:, :, lanes] = cx_tail
        h_carry[:, lanes] = state

    for j in range(N_COL_BLOCKS):
        upcoming = (j + 1, h_scr) if j + 1 < N_COL_BLOCKS else (0, h_next_scr)
        mix_parts = mix(j)
        project(*upcoming, 0)
        next(mix_parts)
        project(*upcoming, 1)
        next(mix_parts, None)

    res = p_scr.at[(N_COL_BLOCKS - 1) % P_SLOTS]
    res[:, :D_MODEL] = jnp.dot(m_scr[...], w_out_ref[...], preferred_element_type=jnp.float32)
    for c in range(n_row_chunks):
        r0 = c * rows
        out = residual_chunk(x_ref, c) + gate * res[r0:r0 + rows, :D_MODEL]
        if final_norm:
            inv_o = lax.rsqrt(jnp.mean(out * out, axis=-1, keepdims=True) + EPS)
            out = (out * inv_o) * fgain_ref[...]
        if unpermute_out:
            out = _permute_chunk(perm_scr, c, out, dyn_zero)
        o_ref[0, r0:r0 + rows, :] = out
    h_scr[...] = h_next_scr[...]


def _layer_resident(arr, layer):
    index = (layer,) + (0,) * (arr.ndim - 1)
    return pl.BlockSpec((1,) + arr.shape[1:], lambda i: index, pipeline_mode=pl.Buffered(1))


def _tile_index(i, *, seq_tiles, last, ahead):
    t = jnp.minimum(i + ahead, last)
    return (t // seq_tiles, t % seq_tiles, 0)


def _layer(x, layer, stacked, w_in_blocked, w_in_raw, final_gain, *, last):
    bsz, seq, d = x.shape
    tile = SEQ_TILE
    seq_tiles = seq // tile
    n_steps = bsz * seq_tiles
    prep_next = layer + 1 < w_in_raw.shape[0]
    slab = d // n_steps
    tile_spec = functools.partial(_tile_index, seq_tiles=seq_tiles, last=n_steps - 1)
    operands =[(arr, layer) for arr in stacked]
    operands.insert(2, (w_in_blocked[None], 0))
    in_specs = [pl.BlockSpec((1, tile, d), functools.partial(tile_spec, ahead=0)),
                pl.BlockSpec((1, tile, d), functools.partial(tile_spec, ahead=1))]
    inputs = [x, x]
    out_specs = [pl.BlockSpec((1, tile, d), functools.partial(tile_spec, ahead=0))]
    out_shape = [jax.ShapeDtypeStruct(x.shape, x.dtype)]
    if prep_next:
        in_specs.append(pl.BlockSpec((1, slab, w_in_raw.shape[2]), lambda i: (layer + 1, i, 0)))
        inputs.append(w_in_raw)
        out_specs.append(pl.BlockSpec((N_COL_BLOCKS, slab, BLOCK_COLS), lambda i: (0, i, 0)))
        out_shape.append(jax.ShapeDtypeStruct((N_COL_BLOCKS, d, BLOCK_COLS), jnp.bfloat16))
    in_specs += [_layer_resident(arr, index) for arr, index in operands]
    in_specs.append(pl.BlockSpec(final_gain.shape, lambda i: (0, 0), pipeline_mode=pl.Buffered(1)))
    inputs += [arr for arr, _ in operands] + [final_gain]
    outs = pl.pallas_call(
        functools.partial(_layer_kernel, final_norm=last, seq_tiles=seq_tiles,
                          prep_next=prep_next, permute_in=(layer == 0), unpermute_out=last),
        grid=(n_steps,),
        in_specs=in_specs,
        out_specs=out_specs,
        out_shape=out_shape,
        scratch_shapes=[
            pltpu.VMEM((tile, d), jnp.bfloat16),
            pltpu.VMEM((tile, d), jnp.bfloat16),
            pltpu.VMEM((tile, d), jnp.bfloat16),
            pltpu.VMEM((P_SLOTS, tile, BLOCK_COLS), jnp.float32),
            pltpu.VMEM((2, tile, LANES), jnp.float32),
            pltpu.VMEM((2, tile, LANES), jnp.bfloat16),
            pltpu.VMEM((2, tile, LANES), jnp.bfloat16),
            pltpu.VMEM((2, tile, 2 * LANES), jnp.float32),
            pltpu.VMEM((2, SGU_CHUNK, tile), jnp.float32),
            pltpu.VMEM((SGU_HEADS, SGU_CHUNK, SGU_CHUNK), jnp.bfloat16),
            pltpu.VMEM((2, tile // ROW_CHUNK * PERM_ROWS, LANES), jnp.float32),
            pltpu.VMEM((CONV_K - 1, SUBLANES, d), jnp.float32),
            pltpu.VMEM((LRU_CONV_K - 1, SUBLANES, d), jnp.float32),
            pltpu.VMEM((1, d), jnp.float32),
        ],
        compiler_params=pltpu.CompilerParams(
            dimension_semantics=("arbitrary",),
            vmem_limit_bytes=VMEM_LIMIT_BYTES),
        name="hybrid_mixer_layer",
    )(*inputs)
    return (outs[0], outs[1]) if prep_next else (outs[0], None)


def _block_columns_kernel(*refs):
    o_ref = refs[-1]
    for k in range(N_SLICES):
        o_ref[0, :, k * LANES:(k + 1) * LANES] = refs[k][0].astype(jnp.bfloat16)


def _slice_block_index(j, *, layer, k):
    return (layer, 0, k * N_COL_BLOCKS + j)


def _block_columns(w_in, layer):
    _, d, _ = w_in.shape
    return pl.pallas_call(
        _block_columns_kernel,
        grid=(N_COL_BLOCKS,),
        in_specs=[pl.BlockSpec((1, d, LANES), functools.partial(_slice_block_index, layer=layer, k=k))
                  for k in range(N_SLICES)],
        out_specs=pl.BlockSpec((1, d, BLOCK_COLS), lambda j: (j, 0, 0)),
        out_shape=jax.ShapeDtypeStruct((N_COL_BLOCKS, d, BLOCK_COLS), jnp.bfloat16),
        name="block_columns_bf16",
    )(*([w_in] * N_SLICES))


def _permute_positions(arr, axis):
    axis = axis % arr.ndim
    shape = arr.shape
    split = arr.reshape(shape[:axis] + (SGU_CHUNK // ROW_CHUNK, SUBLANES, SUBLANES) + shape[axis + 1:])
    return jnp.swapaxes(split, axis + 1, axis + 2).reshape(shape)


def _pair_block_diag(w_a, w_x):
    def pairs(w):
        w = w.reshape(w.shape[0], LRU_HEADS // 2, 2, LRU_HEAD_DIM, LRU_HEAD_DIM)
        zero = jnp.zeros_like(w[:, :, 0])
        top = jnp.concatenate([w[:, :, 0], zero], axis=-1)
        bottom = jnp.concatenate([zero, w[:, :, 1]], axis=-1)
        return jnp.concatenate([top, bottom], axis=-2)
    return jnp.concatenate([pairs(w_a), pairs(w_x)], axis=-1)


def kernel(x, c, norm_gain, w_mod, b_mod, w_in, w_out, conv_a_w, sgu_w, sgu_b, lru_conv_w,
           lru_conv_b, lru_wa, lru_ba, lru_wx, lru_bx, lru_lambda, final_gain):
    depth = w_in.shape[0]
    d = x.shape[-1]
    stacked = (
        _modulation(c, w_mod, b_mod),
        norm_gain.reshape(depth, 1, d),
        w_out.astype(jnp.bfloat16),
        conv_a_w,
        sgu_w,
        jnp.broadcast_to(_permute_positions(sgu_b, -1)[..., None], sgu_b.shape + (LANES,)),
        lru_conv_w,
        lru_conv_b.reshape(depth, 1, d),
        _pair_block_diag(lru_wa, lru_wx).astype(jnp.bfloat16),
        lru_ba.reshape(depth, 1, d),
        lru_bx.reshape(depth, 1, d),
        lru_lambda.reshape(depth, 1, d),
    )
    w_in_blocked = _block_columns(w_in, 0)
    for l in range(depth):
        x, w_in_blocked = _layer(x, l, stacked, w_in_blocked, w_in, final_gain.reshape(1, d),
                                 last=(l == depth - 1))
    return x
```

```python
import functools
import math

import jax
import jax.numpy as jnp
from jax import lax
from jax.experimental import pallas as pl
from jax.experimental.pallas import tpu as pltpu

D_MODEL = 1024
SGU_CHUNK = 128
SGU_HEADS = 8
LRU_HEADS = 16
LRU_HEAD_DIM = 64
CONV_K = 3
LRU_CONV_K = 4
LRU_C = 8.0
EPS = 1e-6
LOG2_E = math.log2(math.e)

SUBLANES = 8
LANES = 128
N_SLICES = 12
(A_X, A_B, A_C, A_Z, S_U, S_V, S_Z, R_X, R_Z, G_A, G_S, G_R) = range(N_SLICES)
N_COL_BLOCKS = D_MODEL // LANES
BLOCK_COLS = N_SLICES * LANES
SEQ_TILE = 512
P_SLOTS = 2
ROW_CHUNK = SUBLANES * SUBLANES
PERM_PITCH = SUBLANES + 1
PERM_ROWS = SUBLANES * PERM_PITCH
VMEM_LIMIT_BYTES = 60 * 1024 * 1024


def _sigmoid(v):
    return 1.0 / (1.0 + jnp.exp2(v * (-LOG2_E)))


def _silu(v):
    return v * _sigmoid(v)


def _mod_kernel(c_ref, w_ref, b_ref, o_ref):
    c = c_ref[...]
    o_ref[0, 0] = jnp.dot(_silu(c), w_ref[0], preferred_element_type=jnp.float32) + b_ref[0, 0]


def _modulation(c, w_mod, b_mod):
    depth, d, _ = w_mod.shape
    bsz = c.shape[0]
    return pl.pallas_call(
        _mod_kernel,
        grid=(depth, 3),
        in_specs=[
            pl.BlockSpec((bsz, d), lambda l, k: (0, 0)),
            pl.BlockSpec((1, d, d), lambda l, k: (l, 0, k)),
            pl.BlockSpec((1, 1, 1, d), lambda l, k: (l, k, 0, 0)),
        ],
        out_specs=pl.BlockSpec((1, 1, bsz, d), lambda l, k: (l, k, 0, 0)),
        out_shape=jax.ShapeDtypeStruct((depth, 3, bsz, d), jnp.float32),
        name="adaln_modulation",
    )(c, w_mod, b_mod.reshape(depth, 3, 1, d))


def _permute_rows(scr, chunk, v3, dyn_zero):
    base = chunk * PERM_ROWS
    for g in range(SUBLANES):
        scr[pl.ds(base + g, SUBLANES, stride=PERM_PITCH), :] = v3[g]
    return jnp.stack([scr[pl.ds(dyn_zero + (base + PERM_PITCH * r), SUBLANES), :]
                      for r in range(SUBLANES)], axis=0)


def _permute_chunk(scr, chunk, xs, dyn_zero):
    cols = []
    for l in range(xs.shape[1] // LANES):
        v3 = xs[:, l * LANES:(l + 1) * LANES].reshape(SUBLANES, SUBLANES, LANES)
        plane = scr.at[l % scr.shape[0]]
        cols.append(_permute_rows(plane, chunk, v3, dyn_zero).reshape(ROW_CHUNK, LANES))
    return jnp.concatenate(cols, axis=1)


def _causal_conv_permuted(vp3, prev_tail, w_ref, col):
    k_taps = w_ref.shape[0]
    n_prev = k_taps - 1
    tail = vp3[SUBLANES - n_prev:]
    sublane = lax.broadcasted_iota(jnp.int32, tail.shape, 1)
    before = jnp.where(sublane >= 1, pltpu.roll(tail, 1, axis=1), pltpu.roll(prev_tail, 1, axis=1))
    ext = jnp.concatenate([before, vp3], axis=0)
    acc = w_ref[k_taps - 1:k_taps, col:col + LANES] * vp3
    for j in range(n_prev):
        acc = acc + w_ref[j:j + 1, col:col + LANES] * ext[j:j + SUBLANES]
    return acc, tail


def _linear_scan_permuted(ap3, bp3, h):
    a_cum = [ap3[0]]
    b_cum = [bp3[0]]
    for r in range(1, SUBLANES):
        b_cum.append(ap3[r] * b_cum[-1] + bp3[r])
        a_cum.append(ap3[r] * a_cum[-1])
    alpha, beta = a_cum[-1], b_cum[-1]
    sublane = lax.broadcasted_iota(jnp.int32, alpha.shape, 0)
    for k in (1, 2, 4):
        keep = sublane >= k
        alpha_prev = jnp.where(keep, pltpu.roll(alpha, k, axis=0), 1.0)
        beta_prev = jnp.where(keep, pltpu.roll(beta, k, axis=0), 0.0)
        beta = alpha * beta_prev + beta
        alpha = alpha * alpha_prev
    after = alpha * h + beta
    entering = jnp.where(sublane >= 1, pltpu.roll(after, 1, axis=0), h)
    states = [a_cum[r] * entering + b_cum[r] for r in range(SUBLANES)]
    return jnp.stack(states, axis=0), after[SUBLANES - 1:SUBLANES, :]


def _layer_kernel(*refs, final_norm, seq_tiles, prep_next, permute_in, unpermute_out):
    refs = list(refs)
    w_raw_ref = refs.pop(2) if prep_next else None
    w_next_ref = refs.pop(17) if prep_next else None
    (x_ref, x_next_ref, mod_ref, gain_ref, w_in_ref, w_out_ref, conv_a_ref, sgu_w_ref,
     sgu_b_ref, lru_cw_ref, lru_cb_ref, lru_w_ref, lru_ba_ref, lru_bx_ref,
     lam_ref, fgain_ref, o_ref,
     h_scr, h_next_scr, m_scr, p_scr, xc_scr, xcb_scr, vn_scr, pre_scr, z_scr, sgu_wp_scr,
     perm_scr, cx_carry, rx_carry, h_carry) = refs
    (mod_ref, gain_ref, w_in_ref, w_out_ref, conv_a_ref, sgu_w_ref, sgu_b_ref, lru_cw_ref,
     lru_cb_ref, lru_w_ref, lru_ba_ref, lru_bx_ref, lam_ref) = (
        r.at[0] for r in (mod_ref, gain_ref, w_in_ref, w_out_ref, conv_a_ref, sgu_w_ref,
                          sgu_b_ref, lru_cw_ref, lru_cb_ref, lru_w_ref, lru_ba_ref, lru_bx_ref,
                          lam_ref))
    tile = x_ref.shape[1]
    rows = ROW_CHUNK
    n_row_chunks = tile // rows
    n_sgu_chunks = tile // SGU_CHUNK
    step = pl.program_id(0)
    dyn_zero = step // pl.num_programs(0)
    b = step // seq_tiles
    b_next = jnp.minimum(step + 1, pl.num_programs(0) - 1) // seq_tiles

    @pl.when(step % seq_tiles == 0)
    def _():
        cx_carry[...] = jnp.zeros_like(cx_carry)
        rx_carry[...] = jnp.zeros_like(rx_carry)
        h_carry[...] = jnp.zeros_like(h_carry)

    if prep_next:
        for jb in range(N_COL_BLOCKS):
            for k in range(N_SLICES):
                c0 = k * D_MODEL + jb * LANES
                w_next_ref[jb, :, k * LANES:(k + 1) * LANES] = (
                    w_raw_ref[0, :, c0:c0 + LANES].astype(jnp.bfloat16))

    def residual_chunk(src_ref, c):
        xs = src_ref[0, c * rows:(c + 1) * rows, :]
        return _permute_chunk(perm_scr, c, xs, dyn_zero) if permute_in else xs

    def modulated_norm(src_ref, batch, dst_ref):
        shift = mod_ref[0, pl.ds(batch, 1), :]
        scale = mod_ref[1, pl.ds(batch, 1), :]
        gain_scale = gain_ref[...] * (1.0 + scale)
        for c in range(n_row_chunks):
            r0 = c * rows
            xs = residual_chunk(src_ref, c)
            inv = lax.rsqrt(jnp.mean(xs * xs, axis=-1, keepdims=True) + EPS)
            dst_ref[r0:r0 + rows, :] = ((xs * inv) * gain_scale + shift).astype(jnp.bfloat16)

    def project(j, operand_ref, half):
        cols = slice(0, BLOCK_COLS // 6) if half == 0 else slice(BLOCK_COLS // 6, BLOCK_COLS)
        p_scr[j % P_SLOTS, :, cols] = jnp.dot(operand_ref[...], w_in_ref[j, :, cols],
                                              preferred_element_type=jnp.float32)

    def position_of(row):
        return (row & -ROW_CHUNK) | ((row & (SUBLANES - 1)) * SUBLANES) | (
            (row // SUBLANES) & (SUBLANES - 1))

    @pl.when(step == 0)
    def _():
        modulated_norm(x_ref, b, h_scr)
        project(0, h_scr, 0)
        project(0, h_scr, 1)
        row = lax.broadcasted_iota(jnp.int32, (SGU_CHUNK, SGU_CHUNK), 0)
        col = lax.broadcasted_iota(jnp.int32, (SGU_CHUNK, SGU_CHUNK), 1)
        select_rows = (col == position_of(row)).astype(jnp.bfloat16)
        select_cols = (row == position_of(col)).astype(jnp.bfloat16)
        for head in range(SGU_HEADS):
            w = jnp.where(row >= col, sgu_w_ref[head], 0.0).astype(jnp.bfloat16)
            w = jnp.dot(w, select_cols, preferred_element_type=jnp.float32).astype(jnp.bfloat16)
            sgu_wp_scr[head] = jnp.dot(select_rows, w, preferred_element_type=jnp.float32
                                       ).astype(jnp.bfloat16)

    modulated_norm(x_next_ref, b_next, h_next_scr)
    gate = mod_ref[2, pl.ds(b, 1), :]

    def mix(j):
        col = j * LANES
        lanes = slice(col, col + LANES)
        p = p_scr.at[j % P_SLOTS]
        xc_s, xcb_s, vn_s, pre_s, z_s = (
            r.at[j % 2] for r in (xc_scr, xcb_scr, vn_scr, pre_scr, z_scr))

        def sl(k, r0):
            return p[r0:r0 + rows, k * LANES:(k + 1) * LANES]

        def sl3(k, r0):
            return sl(k, r0).reshape(SUBLANES, SUBLANES, LANES)

        rx_tail = rx_carry[:, :, lanes]
        for c in range(n_row_chunks):
            r0 = c * rows
            xcp, rx_tail = _causal_conv_permuted(sl3(R_X, r0), rx_tail, lru_cw_ref, col)
            xc = (xcp + lru_cb_ref[:, lanes]).reshape(rows, LANES)
            xc_s[r0:r0 + rows, :] = xc
            xcb_s[r0:r0 + rows, :] = xc.astype(jnp.bfloat16)
            v = sl(S_V, r0)
            vc = v - jnp.mean(v, axis=-1, keepdims=True)
            var = jnp.mean(vc * vc, axis=-1, keepdims=True)
            vn_s[r0:r0 + rows, :] = (vc * lax.rsqrt(var + EPS)).astype(jnp.bfloat16)
        rx_carry[:, :, lanes] = rx_tail

        pre_s[...] = jnp.dot(xcb_s[...], lru_w_ref[j], preferred_element_type=jnp.float32)
        vn_wide = jnp.concatenate(
            [vn_s[n * SGU_CHUNK:(n + 1) * SGU_CHUNK, :] for n in range(n_sgu_chunks)], axis=1)
        z_s[...] = jnp.dot(sgu_wp_scr[j], vn_wide, preferred_element_type=jnp.float32)
        yield

        lam = lam_ref[:, lanes]
        softplus_neg_lam = jnp.maximum(-lam, 0.0) + jnp.log1p(jnp.exp(-jnp.abs(lam)))
        log2_a_per_r = (-LRU_C * LOG2_E) * softplus_neg_lam
        cx_tail = cx_carry[:, :, lanes]
        state = h_carry[:, lanes]
        for c in range(n_row_chunks):
            r0 = c * rows
            chunk, q0 = divmod(r0, SGU_CHUNK)

            convp, cx_tail = _causal_conv_permuted(sl3(A_C, r0) * sl3(A_X, r0), cx_tail,
                                                   conv_a_ref, col)
            conv = convp.reshape(rows, LANES)
            merged = _sigmoid(sl(G_A, r0)) * ((_silu(sl(A_Z, r0)) * sl(A_B, r0)) * conv)

            z = (z_s[q0:q0 + rows, chunk * LANES:(chunk + 1) * LANES]
                 + sgu_b_ref[j, q0:q0 + rows, :])
            merged = merged + _sigmoid(sl(G_S, r0)) * ((_silu(sl(S_Z, r0)) * sl(S_U, r0)) * z)

            r =_sigmoid(pre_s[r0:r0 + rows, :LANES] + lru_ba_ref[:, lanes])
            i = _sigmoid(pre_s[r0:r0 + rows, LANES:] + lru_bx_ref[:, lanes])
            a = jnp.exp2(r * log2_a_per_r)
            y = 1.0 - a * a
            bb = (y * lax.rsqrt(jnp.maximum(y, 1e-30))) * (i * xc_s[r0:r0 + rows, :])
            hsp, state = _linear_scan_permuted(a.reshape(SUBLANES, SUBLANES, LANES),
                                               bb.reshape(SUBLANES, SUBLANES, LANES), state)
            merged = merged + _sigmoid(sl(G_R, r0)) * (
                _silu(sl(R_Z, r0)) * hsp.reshape(rows, LANES))

            m_scr[r0:r0 + rows, lanes] = merged.astype(jnp.bfloat16)
        cx_carry[:, :, lanes] = cx_tail
        h_carry[:, lanes] = state

    for j in range(N_COL_BLOCKS):
        upcoming = (j + 1, h_scr) if j + 1 < N_COL_BLOCKS else (0, h_next_scr)
        mix_parts = mix(j)
        next(mix_parts)
        project(*upcoming, 0)
        project(*upcoming, 1)
        next(mix_parts, None)

    res = p_scr.at[(N_COL_BLOCKS - 1) % P_SLOTS]
    res[:, :D_MODEL] = jnp.dot(m_scr[...], w_out_ref[...], preferred_element_type=jnp.float32)
    for c in range(n_row_chunks):
        r0 = c * rows
        out = residual_chunk(x_ref, c) + gate * res[r0:r0 + rows, :D_MODEL]
        if final_norm:
            inv_o = lax.rsqrt(jnp.mean(out * out, axis=-1, keepdims=True) + EPS)
            out = (out * inv_o) * fgain_ref[...]
        if unpermute_out:
            out = _permute_chunk(perm_scr, c, out, dyn_zero)
        o_ref[0, r0:r0 + rows, :] = out
    h_scr[...] = h_next_scr[...]


def _layer_resident(arr, layer):
    index = (layer,) + (0,) * (arr.ndim - 1)
    return pl.BlockSpec((1,) + arr.shape[1:], lambda i: index, pipeline_mode=pl.Buffered(1))


def _tile_index(i, *, seq_tiles, last, ahead):
    t = jnp.minimum(i + ahead, last)
    return (t // seq_tiles, t % seq_tiles, 0)


def _layer(x, layer, stacked, w_in_blocked, w_in_raw, final_gain, *, last):
    bsz, seq, d = x.shape
    tile = SEQ_TILE
    seq_tiles = seq // tile
    n_steps = bsz * seq_tiles
    prep_next = layer + 1 < w_in_raw.shape[0]
    slab = d // n_steps
    tile_spec = functools.partial(_tile_index, seq_tiles=seq_tiles, last=n_steps - 1)
    operands =[(arr, layer) for arr in stacked]
    operands.insert(2, (w_in_blocked[None], 0))
    in_specs = [pl.BlockSpec((1, tile, d), functools.partial(tile_spec, ahead=0)),
                pl.BlockSpec((1, tile, d), functools.partial(tile_spec, ahead=1))]
    inputs = [x, x]
    out_specs = [pl.BlockSpec((1, tile, d), functools.partial(tile_spec, ahead=0))]
    out_shape = [jax.ShapeDtypeStruct(x.shape, x.dtype)]
    if prep_next:
        in_specs.append(pl.BlockSpec((1, slab, w_in_raw.shape[2]), lambda i: (layer + 1, i, 0)))
        inputs.append(w_in_raw)
        out_specs.append(pl.BlockSpec((N_COL_BLOCKS, slab, BLOCK_COLS), lambda i: (0, i, 0)))
        out_shape.append(jax.ShapeDtypeStruct((N_COL_BLOCKS, d, BLOCK_COLS), jnp.bfloat16))
    in_specs += [_layer_resident(arr, index) for arr, index in operands]
    in_specs.append(pl.BlockSpec(final_gain.shape, lambda i: (0, 0), pipeline_mode=pl.Buffered(1)))
    inputs += [arr for arr, _ in operands] + [final_gain]
    outs = pl.pallas_call(
        functools.partial(_layer_kernel, final_norm=last, seq_tiles=seq_tiles,
                          prep_next=prep_next, permute_in=(layer == 0), unpermute_out=last),
        grid=(n_steps,),
        in_specs=in_specs,
        out_specs=out_specs,
        out_shape=out_shape,
        scratch_shapes=[
            pltpu.VMEM((tile, d), jnp.bfloat16),
            pltpu.VMEM((tile, d), jnp.bfloat16),
            pltpu.VMEM((tile, d), jnp.bfloat16),
            pltpu.VMEM((P_SLOTS, tile, BLOCK_COLS), jnp.float32),
            pltpu.VMEM((2, tile, LANES), jnp.float32),
            pltpu.VMEM((2, tile, LANES), jnp.bfloat16),
            pltpu.VMEM((2, tile, LANES), jnp.bfloat16),
            pltpu.VMEM((2, tile, 2 * LANES), jnp.float32),
            pltpu.VMEM((2, SGU_CHUNK, tile), jnp.float32),
            pltpu.VMEM((SGU_HEADS, SGU_CHUNK, SGU_CHUNK), jnp.bfloat16),
            pltpu.VMEM((2, tile // ROW_CHUNK * PERM_ROWS, LANES), jnp.float32),
            pltpu.VMEM((CONV_K - 1, SUBLANES, d), jnp.float32),
            pltpu.VMEM((LRU_CONV_K - 1, SUBLANES, d), jnp.float32),
            pltpu.VMEM((1, d), jnp.float32),
        ],
        compiler_params=pltpu.CompilerParams(
            dimension_semantics=("arbitrary",),
            vmem_limit_bytes=VMEM_LIMIT_BYTES),
        name="hybrid_mixer_layer",
    )(*inputs)
    return (outs[0], outs[1]) if prep_next else (outs[0], None)


def _block_columns_kernel(*refs):
    o_ref = refs[-1]
    for k in range(N_SLICES):
        o_ref[0, :, k * LANES:(k + 1) * LANES] = refs[k][0].astype(jnp.bfloat16)


def _slice_block_index(j, *, layer, k):
    return (layer, 0, k * N_COL_BLOCKS + j)


def _block_columns(w_in, layer):
    _, d, _ = w_in.shape
    return pl.pallas_call(
        _block_columns_kernel,
        grid=(N_COL_BLOCKS,),
        in_specs=[pl.BlockSpec((1, d, LANES), functools.partial(_slice_block_index, layer=layer, k=k))
                  for k in range(N_SLICES)],
        out_specs=pl.BlockSpec((1, d, BLOCK_COLS), lambda j: (j, 0, 0)),
        out_shape=jax.ShapeDtypeStruct((N_COL_BLOCKS, d, BLOCK_COLS), jnp.bfloat16),
        name="block_columns_bf16",
    )(*([w_in] * N_SLICES))


def _permute_positions(arr, axis):
    axis = axis % arr.ndim
    shape = arr.shape
    split = arr.reshape(shape[:axis] + (SGU_CHUNK // ROW_CHUNK, SUBLANES, SUBLANES) + shape[axis + 1:])
    return jnp.swapaxes(split, axis + 1, axis + 2).reshape(shape)


def _pair_block_diag(w_a, w_x):
    def pairs(w):
        w = w.reshape(w.shape[0], LRU_HEADS // 2, 2, LRU_HEAD_DIM, LRU_HEAD_DIM)
        zero = jnp.zeros_like(w[:, :, 0])
        top = jnp.concatenate([w[:, :, 0], zero], axis=-1)
        bottom = jnp.concatenate([zero, w[:, :, 1]], axis=-1)
        return jnp.concatenate([top, bottom], axis=-2)
    return jnp.concatenate([pairs(w_a), pairs(w_x)], axis=-1)


def kernel(x, c, norm_gain, w_mod, b_mod, w_in, w_out, conv_a_w, sgu_w, sgu_b, lru_conv_w,
           lru_conv_b, lru_wa, lru_ba, lru_wx, lru_bx, lru_lambda, final_gain):
    depth = w_in.shape[0]
    d = x.shape[-1]
    stacked = (
        _modulation(c, w_mod, b_mod),
        norm_gain.reshape(depth, 1, d),
        w_out.astype(jnp.bfloat16),
        conv_a_w,
        sgu_w,
        jnp.broadcast_to(_permute_positions(sgu_b, -1)[..., None], sgu_b.shape + (LANES,)),
        lru_conv_w,
        lru_conv_b.reshape(depth, 1, d),
        _pair_block_diag(lru_wa, lru_wx).astype(jnp.bfloat16),
        lru_ba.reshape(depth, 1, d),
        lru_bx.reshape(depth, 1, d),
        lru_lambda.reshape(depth, 1, d),
    )
    w_in_blocked = _block_columns(w_in, 0)
    for l in range(depth):
        x, w_in_blocked = _layer(x, l, stacked, w_in_blocked, w_in, final_gain.reshape(1, d),
                                 last=(l == depth - 1))
    return x
```

```python
import functools
import math

import jax
import jax.numpy as jnp
from jax import lax
from jax.experimental import pallas as pl
from jax.experimental.pallas import tpu as pltpu

D_MODEL = 1024
SGU_CHUNK = 128
SGU_HEADS = 8
LRU_HEADS = 16
LRU_HEAD_DIM = 64
CONV_K = 3
LRU_CONV_K = 4
LRU_C = 8.0
EPS = 1e-6
LOG2_E = math.log2(math.e)

SUBLANES = 8
LANES = 128
N_SLICES = 12
(A_X, A_B, A_C, A_Z, S_U, S_V, S_Z, R_X, R_Z, G_A, G_S, G_R) = range(N_SLICES)
N_COL_BLOCKS = D_MODEL // LANES
BLOCK_COLS = N_SLICES * LANES
SEQ_TILE = 512
P_SLOTS = 2
ROW_CHUNK = SUBLANES * SUBLANES
PERM_PITCH = SUBLANES + 1
PERM_ROWS = SUBLANES * PERM_PITCH
VMEM_LIMIT_BYTES = 60 * 1024 * 1024


def _sigmoid(v):
    return 1.0 / (1.0 + jnp.exp2(v * (-LOG2_E)))


def _silu(v):
    return v * _sigmoid(v)


def _mod_kernel(c_ref, w_ref, b_ref, o_ref):
    c = c_ref[...]
    o_ref[0, 0] = jnp.dot(_silu(c), w_ref[0], preferred_element_type=jnp.float32) + b_ref[0, 0]


def _modulation(c, w_mod, b_mod):
    depth, d, _ = w_mod.shape
    bsz = c.shape[0]
    return pl.pallas_call(
        _mod_kernel,
        grid=(depth, 3),
        in_specs=[
            pl.BlockSpec((bsz, d), lambda l, k: (0, 0)),
            pl.BlockSpec((1, d, d), lambda l, k: (l, 0, k)),
            pl.BlockSpec((1, 1, 1, d), lambda l, k: (l, k, 0, 0)),
        ],
        out_specs=pl.BlockSpec((1, 1, bsz, d), lambda l, k: (l, k, 0, 0)),
        out_shape=jax.ShapeDtypeStruct((depth, 3, bsz, d), jnp.float32),
        name="adaln_modulation",
    )(c, w_mod, b_mod.reshape(depth, 3, 1, d))


def _permute_rows(scr, chunk, v3, dyn_zero):
    base = chunk * PERM_ROWS
    for g in range(SUBLANES):
        scr[pl.ds(base + g, SUBLANES, stride=PERM_PITCH), :] = v3[g]
    return jnp.stack([scr[pl.ds(dyn_zero + (base + PERM_PITCH * r), SUBLANES), :]
                      for r in range(SUBLANES)], axis=0)


def _permute_chunk(scr, chunk, xs, dyn_zero):
    cols = []
    for l in range(xs.shape[1] // LANES):
        v3 = xs[:, l * LANES:(l + 1) * LANES].reshape(SUBLANES, SUBLANES, LANES)
        plane = scr.at[l % scr.shape[0]]
        cols.append(_permute_rows(plane, chunk, v3, dyn_zero).reshape(ROW_CHUNK, LANES))
    return jnp.concatenate(cols, axis=1)


def _causal_conv_permuted(vp3, prev_tail, w_ref, col):
    k_taps = w_ref.shape[0]
    n_prev = k_taps - 1
    tail = vp3[SUBLANES - n_prev:]
    sublane = lax.broadcasted_iota(jnp.int32, tail.shape, 1)
    before = jnp.where(sublane >= 1, pltpu.roll(tail, 1, axis=1), pltpu.roll(prev_tail, 1, axis=1))
    ext = jnp.concatenate([before, vp3], axis=0)
    acc = w_ref[k_taps - 1:k_taps, col:col + LANES] * vp3
    for j in range(n_prev):
        acc = acc + w_ref[j:j + 1, col:col + LANES] * ext[j:j + SUBLANES]
    return acc, tail


def _linear_scan_permuted(ap3, bp3, h):
    a_cum = [ap3[0]]
    b_cum = [bp3[0]]
    for r in range(1, SUBLANES):
        b_cum.append(ap3[r] * b_cum[-1] + bp3[r])
        a_cum.append(ap3[r] * a_cum[-1])
    alpha, beta = a_cum[-1], b_cum[-1]
    sublane = lax.broadcasted_iota(jnp.int32, alpha.shape, 0)
    for k in (1, 2, 4):
        keep = sublane >= k
        alpha_prev = jnp.where(keep, pltpu.roll(alpha, k, axis=0), 1.0)
        beta_prev = jnp.where(keep, pltpu.roll(beta, k, axis=0), 0.0)
        beta = alpha * beta_prev + beta
        alpha = alpha * alpha_prev
    after = alpha * h + beta
    entering = jnp.where(sublane >= 1, pltpu.roll(after, 1, axis=0), h)
    states = [a_cum[r] * entering + b_cum[r] for r in range(SUBLANES)]
    return jnp.stack(states, axis=0), after[SUBLANES - 1:SUBLANES, :]


def _layer_kernel(*refs, final_norm, seq_tiles, prep_next, permute_in, unpermute_out):
    refs = list(refs)
    w_raw_ref = refs.pop(2) if prep_next else None
    w_next_ref = refs.pop(17) if prep_next else None
    (x_ref, x_next_ref, mod_ref, gain_ref, w_in_ref, w_out_ref, conv_a_ref, sgu_w_ref,
     sgu_b_ref, lru_cw_ref, lru_cb_ref, lru_w_ref, lru_ba_ref, lru_bx_ref,
     lam_ref, fgain_ref, o_ref,
     h_scr, h_next_scr, m_scr, p_scr, xc_scr, xcb_scr, vn_scr, pre_scr, z_scr, sgu_wp_scr,
     perm_scr, cx_carry, rx_carry, h_carry) = refs
    (mod_ref, gain_ref, w_in_ref, w_out_ref, conv_a_ref, sgu_w_ref, sgu_b_ref, lru_cw_ref,
     lru_cb_ref, lru_w_ref, lru_ba_ref, lru_bx_ref, lam_ref) = (
        r.at[0] for r in (mod_ref, gain_ref, w_in_ref, w_out_ref, conv_a_ref, sgu_w_ref,
                          sgu_b_ref, lru_cw_ref, lru_cb_ref, lru_w_ref, lru_ba_ref, lru_bx_ref,
                          lam_ref))
    tile = x_ref.shape[1]
    rows = ROW_CHUNK
    n_row_chunks = tile // rows
    n_sgu_chunks = tile // SGU_CHUNK
    step = pl.program_id(0)
    dyn_zero = step // pl.num_programs(0)
    b = step // seq_tiles
    b_next = jnp.minimum(step + 1, pl.num_programs(0) - 1) // seq_tiles

    @pl.when(step % seq_tiles == 0)
    def _():
        cx_carry[...] = jnp.zeros_like(cx_carry)
        rx_carry[...] = jnp.zeros_like(rx_carry)
        h_carry[...] = jnp.zeros_like(h_carry)

    if prep_next:
        for jb in range(N_COL_BLOCKS):
            for k in range(N_SLICES):
                c0 = k * D_MODEL + jb * LANES
                w_next_ref[jb, :, k * LANES:(k + 1) * LANES] = (
                    w_raw_ref[0, :, c0:c0 + LANES].astype(jnp.bfloat16))

    def residual_chunk(src_ref, c):
        xs = src_ref[0, c * rows:(c + 1) * rows, :]
        return _permute_chunk(perm_scr, c, xs, dyn_zero) if permute_in else xs

    def modulated_norm(src_ref, batch, dst_ref):
        shift = mod_ref[0, pl.ds(batch, 1), :]
        scale = mod_ref[1, pl.ds(batch, 1), :]
        gain_scale = gain_ref[...] * (1.0 + scale)
        for c in range(n_row_chunks):
            r0 = c * rows
            xs = residual_chunk(src_ref, c)
            inv = lax.rsqrt(jnp.mean(xs * xs, axis=-1, keepdims=True) + EPS)
            dst_ref[r0:r0 + rows, :] = ((xs * inv) * gain_scale + shift).astype(jnp.bfloat16)

    def project(j, operand_ref, half):
        cols = slice(0, BLOCK_COLS // 6) if half == 0 else slice(BLOCK_COLS // 6, BLOCK_COLS)
        p_scr[j % P_SLOTS, :, cols] = jnp.dot(operand_ref[...], w_in_ref[j, :, cols],
                                              preferred_element_type=jnp.float32)

    def position_of(row):
        return (row & -ROW_CHUNK) | ((row & (SUBLANES - 1)) * SUBLANES) | (
            (row // SUBLANES) & (SUBLANES - 1))

    @pl.when(step == 0)
    def _():
        modulated_norm(x_ref, b, h_scr)
        project(0, h_scr, 0)
        project(0, h_scr, 1)
        row = lax.broadcasted_iota(jnp.int32, (SGU_CHUNK, SGU_CHUNK), 0)
        col = lax.broadcasted_iota(jnp.int32, (SGU_CHUNK, SGU_CHUNK), 1)
        select_rows = (col == position_of(row)).astype(jnp.bfloat16)
        select_cols = (row == position_of(col)).astype(jnp.bfloat16)
        for head in range(SGU_HEADS):
            w = jnp.where(row >= col, sgu_w_ref[head], 0.0).astype(jnp.bfloat16)
            w = jnp.dot(w, select_cols, preferred_element_type=jnp.float32).astype(jnp.bfloat16)
            sgu_wp_scr[head] = jnp.dot(select_rows, w, preferred_element_type=jnp.float32
                                       ).astype(jnp.bfloat16)

    modulated_norm(x_next_ref, b_next, h_next_scr)
    gate = mod_ref[2, pl.ds(b, 1), :]

    def mix(j):
        col = j * LANES
        lanes = slice(col, col + LANES)
        p = p_scr.at[j % P_SLOTS]
        xc_s, xcb_s, vn_s, pre_s, z_s = (
            r.at[j % 2] for r in (xc_scr, xcb_scr, vn_scr, pre_scr, z_scr))

        def sl(k, r0):
            return p[r0:r0 + rows, k * LANES:(k + 1) * LANES]

        def sl3(k, r0):
            return sl(k, r0).reshape(SUBLANES, SUBLANES, LANES)

        rx_tail = rx_carry[:, :, lanes]
        for c in range(n_row_chunks):
            r0 = c * rows
            xcp, rx_tail = _causal_conv_permuted(sl3(R_X, r0), rx_tail, lru_cw_ref, col)
            xc = (xcp + lru_cb_ref[:, lanes]).reshape(rows, LANES)
            xc_s[r0:r0 + rows, :] = xc
            xcb_s[r0:r0 + rows, :] = xc.astype(jnp.bfloat16)
            v = sl(S_V, r0)
            vc = v - jnp.mean(v, axis=-1, keepdims=True)
            var = jnp.mean(vc * vc, axis=-1, keepdims=True)
            vn_s[r0:r0 + rows, :] = (vc * lax.rsqrt(var + EPS)).astype(jnp.bfloat16)
        rx_carry[:, :, lanes] = rx_tail

        pre_s[...] = jnp.dot(xcb_s[...], lru_w_ref[j], preferred_element_type=jnp.float32)
        vn_wide = jnp.concatenate(
            [vn_s[n * SGU_CHUNK:(n + 1) * SGU_CHUNK, :] for n in range(n_sgu_chunks)], axis=1)
        z_s[...] = jnp.dot(sgu_wp_scr[j], vn_wide, preferred_element_type=jnp.float32)
        yield

        lam = lam_ref[:, lanes]
        softplus_neg_lam = jnp.maximum(-lam, 0.0) + jnp.log1p(jnp.exp(-jnp.abs(lam)))
        log2_a_per_r = (-LRU_C * LOG2_E) * softplus_neg_lam
        cx_tail = cx_carry[:, :, lanes]
        state = h_carry[:, lanes]
        for c in range(n_row_chunks):
            r0 = c * rows
            chunk, q0 = divmod(r0, SGU_CHUNK)

            convp, cx_tail = _causal_conv_permuted(sl3(A_C, r0) * sl3(A_X, r0), cx_tail,
                                                   conv_a_ref, col)
            conv = convp.reshape(rows, LANES)
            merged = _sigmoid(sl(G_A, r0)) * ((_silu(sl(A_Z, r0)) * sl(A_B, r0)) * conv)

            z = (z_s[q0:q0 + rows, chunk * LANES:(chunk + 1) * LANES]
                 + sgu_b_ref[j, q0:q0 + rows, :])
            merged = merged + _sigmoid(sl(G_S, r0)) * ((_silu(sl(S_Z, r0)) * sl(S_U, r0)) * z)

            r =_sigmoid(pre_s[r0:r0 + rows, :LANES] + lru_ba_ref[:, lanes])
            i = _sigmoid(pre_s[r0:r0 + rows, LANES:] + lru_bx_ref[:, lanes])
            a = jnp.exp2(r * log2_a_per_r)
            y = 1.0 - a * a
            bb = (y * lax.rsqrt(jnp.maximum(y, 1e-30))) * (i * xc_s[r0:r0 + rows, :])
            hsp, state = _linear_scan_permuted(a.reshape(SUBLANES, SUBLANES, LANES),
                                               bb.reshape(SUBLANES, SUBLANES, LANES), state)
            merged = merged + _sigmoid(sl(G_R, r0)) * (
                _silu(sl(R_Z, r0)) * hsp.reshape(rows, LANES))

            m_scr[r0:r0 + rows, lanes] = merged.astype(jnp.bfloat16)
        cx_carry[:, :, lanes] = cx_tail
        h_carry[:, lanes] = state

    for j in range(N_COL_BLOCKS):
        upcoming = (j + 1, h_scr) if j + 1 < N_COL_BLOCKS else (0, h_next_scr)
        mix_parts = mix(j)
        project(*upcoming, 0)
        next(mix_parts)
        project(*upcoming, 1)
        if j + 2 == N_COL_BLOCKS:
            h_scr[...] = h_next_scr[...]
        next(mix_parts, None)

    res = p_scr.at[(N_COL_BLOCKS - 1) % P_SLOTS]
    res[:, :D_MODEL] = jnp.dot(m_scr[...], w_out_ref[...], preferred_element_type=jnp.float32)
    for c in range(n_row_chunks):
        r0 = c * rows
        out = residual_chunk(x_ref, c) + gate * res[r0:r0 + rows, :D_MODEL]
        if final_norm:
            inv_o = lax.rsqrt(jnp.mean(out * out, axis=-1, keepdims=True) + EPS)
            out = (out * inv_o) * fgain_ref[...]
        if unpermute_out:
            out = _permute_chunk(perm_scr, c, out, dyn_zero)
        o_ref[0, r0:r0 + rows, :] = out


def _layer_resident(arr, layer):
    index = (layer,) + (0,) * (arr.ndim - 1)
    return pl.BlockSpec((1,) + arr.shape[1:], lambda i: index, pipeline_mode=pl.Buffered(1))


def _tile_index(i, *, seq_tiles, last, ahead):
    t = jnp.minimum(i + ahead, last)
    return (t // seq_tiles, t % seq_tiles, 0)


def _layer(x, layer, stacked, w_in_blocked, w_in_raw, final_gain, *, last):
    bsz, seq, d = x.shape
    tile = SEQ_TILE
    seq_tiles = seq // tile
    n_steps = bsz * seq_tiles
    prep_next = layer + 1 < w_in_raw.shape[0]
    slab = d // n_steps
    tile_spec = functools.partial(_tile_index, seq_tiles=seq_tiles, last=n_steps - 1)
    operands =[(arr, layer) for arr in stacked]
    operands.insert(2, (w_in_blocked[None], 0))
    in_specs = [pl.BlockSpec((1, tile, d), functools.partial(tile_spec, ahead=0)),
                pl.BlockSpec((1, tile, d), functools.partial(tile_spec, ahead=1))]
    inputs = [x, x]
    out_specs = [pl.BlockSpec((1, tile, d), functools.partial(tile_spec, ahead=0))]
    out_shape = [jax.ShapeDtypeStruct(x.shape, x.dtype)]
    if prep_next:
        in_specs.append(pl.BlockSpec((1, slab, w_in_raw.shape[2]), lambda i: (layer + 1, i, 0)))
        inputs.append(w_in_raw)
        out_specs.append(pl.BlockSpec((N_COL_BLOCKS, slab, BLOCK_COLS), lambda i: (0, i, 0)))
        out_shape.append(jax.ShapeDtypeStruct((N_COL_BLOCKS, d, BLOCK_COLS), jnp.bfloat16))
    in_specs += [_layer_resident(arr, index) for arr, index in operands]
    in_specs.append(pl.BlockSpec(final_gain.shape, lambda i: (0, 0), pipeline_mode=pl.Buffered(1)))
    inputs += [arr for arr, _ in operands] + [final_gain]
    outs = pl.pallas_call(
        functools.partial(_layer_kernel, final_norm=last, seq_tiles=seq_tiles,
                          prep_next=prep_next, permute_in=(layer == 0), unpermute_out=last),
        grid=(n_steps,),
        in_specs=in_specs,
        out_specs=out_specs,
        out_shape=out_shape,
        scratch_shapes=[
            pltpu.VMEM((tile, d), jnp.bfloat16),
            pltpu.VMEM((tile, d), jnp.bfloat16),
            pltpu.VMEM((tile, d), jnp.bfloat16),
            pltpu.VMEM((P_SLOTS, tile, BLOCK_COLS), jnp.float32),
            pltpu.VMEM((2, tile, LANES), jnp.float32),
            pltpu.VMEM((2, tile, LANES), jnp.bfloat16),
            pltpu.VMEM((2, tile, LANES), jnp.bfloat16),
            pltpu.VMEM((2, tile, 2 * LANES), jnp.float32),
            pltpu.VMEM((2, SGU_CHUNK, tile), jnp.float32),
            pltpu.VMEM((SGU_HEADS, SGU_CHUNK, SGU_CHUNK), jnp.bfloat16),
            pltpu.VMEM((2, tile // ROW_CHUNK * PERM_ROWS, LANES), jnp.float32),
            pltpu.VMEM((CONV_K - 1, SUBLANES, d), jnp.float32),
            pltpu.VMEM((LRU_CONV_K - 1, SUBLANES, d), jnp.float32),
            pltpu.VMEM((1, d), jnp.float32),
        ],
        compiler_params=pltpu.CompilerParams(
            dimension_semantics=("arbitrary",),
            vmem_limit_bytes=VMEM_LIMIT_BYTES),
        name="hybrid_mixer_layer",
    )(*inputs)
    return (outs[0], outs[1]) if prep_next else (outs[0], None)


def _block_columns_kernel(*refs):
    o_ref = refs[-1]
    for k in range(N_SLICES):
        o_ref[0, :, k * LANES:(k + 1) * LANES] = refs[k][0].astype(jnp.bfloat16)


def _slice_block_index(j, *, layer, k):
    return (layer, 0, k * N_COL_BLOCKS + j)


def _block_columns(w_in, layer):
    _, d, _ = w_in.shape
    return pl.pallas_call(
        _block_columns_kernel,
        grid=(N_COL_BLOCKS,),
        in_specs=[pl.BlockSpec((1, d, LANES), functools.partial(_slice_block_index, layer=layer, k=k))
                  for k in range(N_SLICES)],
        out_specs=pl.BlockSpec((1, d, BLOCK_COLS), lambda j: (j, 0, 0)),
        out_shape=jax.ShapeDtypeStruct((N_COL_BLOCKS, d, BLOCK_COLS), jnp.bfloat16),
        name="block_columns_bf16",
    )(*([w_in] * N_SLICES))


def _permute_positions(arr, axis):
    axis = axis % arr.ndim
    shape = arr.shape
    split = arr.reshape(shape[:axis] + (SGU_CHUNK // ROW_CHUNK, SUBLANES, SUBLANES) + shape[axis + 1:])
    return jnp.swapaxes(split, axis + 1, axis + 2).reshape(shape)


def _pair_block_diag(w_a, w_x):
    def pairs(w):
        w = w.reshape(w.shape[0], LRU_HEADS // 2, 2, LRU_HEAD_DIM, LRU_HEAD_DIM)
        zero = jnp.zeros_like(w[:, :, 0])
        top = jnp.concatenate([w[:, :, 0], zero], axis=-1)
        bottom = jnp.concatenate([zero, w[:, :, 1]], axis=-1)
        return jnp.concatenate([top, bottom], axis=-2)
    return jnp.concatenate([pairs(w_a), pairs(w_x)], axis=-1)


def kernel(x, c, norm_gain, w_mod, b_mod, w_in, w_out, conv_a_w, sgu_w, sgu_b, lru_conv_w,
           lru_conv_b, lru_wa, lru_ba, lru_wx, lru_bx, lru_lambda, final_gain):
    depth = w_in.shape[0]
    d = x.shape[-1]
    stacked = (
        _modulation(c, w_mod, b_mod),
        norm_gain.reshape(depth, 1, d),
        w_out.astype(jnp.bfloat16),
        conv_a_w,
        sgu_w,
        jnp.broadcast_to(_permute_positions(sgu_b, -1)[..., None], sgu_b.shape + (LANES,)),
        lru_conv_w,
        lru_conv_b.reshape(depth, 1, d),
        _pair_block_diag(lru_wa, lru_wx).astype(jnp.bfloat16),
        lru_ba.reshape(depth, 1, d),
        lru_bx.reshape(depth, 1, d),
        lru_lambda.reshape(depth, 1, d),
    )
    w_in_blocked = _block_columns(w_in, 0)
    for l in range(depth):
        x, w_in_blocked = _layer(x, l, stacked, w_in_blocked, w_in, final_gain.reshape(1, d),
                                 last=(l == depth - 1))
    return x
```

```python
import functools
import math

import jax
import jax.numpy as jnp
from jax import lax
from jax.experimental import pallas as pl
from jax.experimental.pallas import tpu as pltpu

D_MODEL = 1024
SGU_CHUNK = 128
SGU_HEADS = 8
LRU_HEADS = 16
LRU_HEAD_DIM = 64
CONV_K = 3
LRU_CONV_K = 4
LRU_C = 8.0
EPS = 1e-6
LOG2_E = math.log2(math.e)

SUBLANES = 8
LANES = 128
N_SLICES = 12
(A_X, A_B, A_C, A_Z, S_U, S_V, S_Z, R_X, R_Z, G_A, G_S, G_R) = range(N_SLICES)
N_COL_BLOCKS = D_MODEL // LANES
BLOCK_COLS = N_SLICES * LANES
SEQ_TILE = 512
P_SLOTS = 2
ROW_CHUNK = SUBLANES * SUBLANES
PERM_PITCH = SUBLANES + 1
PERM_ROWS = SUBLANES * PERM_PITCH
VMEM_LIMIT_BYTES = 60 * 1024 * 1024


def _sigmoid(v):
    return 1.0 / (1.0 + jnp.exp2(v * (-LOG2_E)))


def _silu(v):
    return v * _sigmoid(v)


def _mod_kernel(c_ref, w_ref, b_ref, o_ref):
    c = c_ref[...]
    o_ref[0, 0] = jnp.dot(_silu(c), w_ref[0], preferred_element_type=jnp.float32) + b_ref[0, 0]


def _modulation(c, w_mod, b_mod):
    depth, d, _ = w_mod.shape
    bsz = c.shape[0]
    return pl.pallas_call(
        _mod_kernel,
        grid=(depth, 3),
        in_specs=[
            pl.BlockSpec((bsz, d), lambda l, k: (0, 0)),
            pl.BlockSpec((1, d, d), lambda l, k: (l, 0, k)),
            pl.BlockSpec((1, 1, 1, d), lambda l, k: (l, k, 0, 0)),
        ],
        out_specs=pl.BlockSpec((1, 1, bsz, d), lambda l, k: (l, k, 0, 0)),
        out_shape=jax.ShapeDtypeStruct((depth, 3, bsz, d), jnp.float32),
        name="adaln_modulation",
    )(c, w_mod, b_mod.reshape(depth, 3, 1, d))


def _permute_rows(scr, chunk, v3, dyn_zero):
    base = chunk * PERM_ROWS
    for g in range(SUBLANES):
        scr[pl.ds(base + g, SUBLANES, stride=PERM_PITCH), :] = v3[g]
    return jnp.stack([scr[pl.ds(dyn_zero + (base + PERM_PITCH * r), SUBLANES), :]
                      for r in range(SUBLANES)], axis=0)


def _permute_chunk(scr, chunk, xs, dyn_zero):
    cols = []
    for l in range(xs.shape[1] // LANES):
        v3 = xs[:, l * LANES:(l + 1) * LANES].reshape(SUBLANES, SUBLANES, LANES)
        plane = scr.at[l % scr.shape[0]]
        cols.append(_permute_rows(plane, chunk, v3, dyn_zero).reshape(ROW_CHUNK, LANES))
    return jnp.concatenate(cols, axis=1)


def _causal_conv_permuted(vp3, prev_tail, w_ref, col):
    k_taps = w_ref.shape[0]
    n_prev = k_taps - 1
    tail = vp3[SUBLANES - n_prev:]
    sublane = lax.broadcasted_iota(jnp.int32, tail.shape, 1)
    before = jnp.where(sublane >= 1, pltpu.roll(tail, 1, axis=1), pltpu.roll(prev_tail, 1, axis=1))
    ext = jnp.concatenate([before, vp3], axis=0)
    acc = w_ref[k_taps - 1:k_taps, col:col + LANES] * vp3
    for j in range(n_prev):
        acc = acc + w_ref[j:j + 1, col:col + LANES] * ext[j:j + SUBLANES]
    return acc, tail


def _linear_scan_permuted(ap3, bp3, h):
    a_cum = [ap3[0]]
    b_cum = [bp3[0]]
    for r in range(1, SUBLANES):
        b_cum.append(ap3[r] * b_cum[-1] + bp3[r])
        a_cum.append(ap3[r] * a_cum[-1])
    alpha, beta = a_cum[-1], b_cum[-1]
    sublane = lax.broadcasted_iota(jnp.int32, alpha.shape, 0)
    for k in (1, 2, 4):
        keep = sublane >= k
        alpha_prev = jnp.where(keep, pltpu.roll(alpha, k, axis=0), 1.0)
        beta_prev = jnp.where(keep, pltpu.roll(beta, k, axis=0), 0.0)
        beta = alpha * beta_prev + beta
        alpha = alpha * alpha_prev
    after = alpha * h + beta
    entering = jnp.where(sublane >= 1, pltpu.roll(after, 1, axis=0), h)
    states = [a_cum[r] * entering + b_cum[r] for r in range(SUBLANES)]
    return jnp.stack(states, axis=0), after[SUBLANES - 1:SUBLANES, :]


def _layer_kernel(*refs, final_norm, seq_tiles, prep_next, permute_in, unpermute_out):
    refs = list(refs)
    w_raw_ref = refs.pop(2) if prep_next else None
    w_next_ref = refs.pop(17) if prep_next else None
    (x_ref, x_next_ref, mod_ref, gain_ref, w_in_ref, w_out_ref, conv_a_ref, sgu_w_ref,
     sgu_b_ref, lru_cw_ref, lru_cb_ref, lru_w_ref, lru_ba_ref, lru_bx_ref,
     lam_ref, fgain_ref, o_ref,
     h_scr, h_next_scr, m_scr, p_scr, xc_scr, xcb_scr, vn_scr, pre_scr, z_scr, sgu_wp_scr,
     perm_scr, cx_carry, rx_carry, h_carry) = refs
    (mod_ref, gain_ref, w_in_ref, w_out_ref, conv_a_ref, sgu_w_ref, sgu_b_ref, lru_cw_ref,
     lru_cb_ref, lru_w_ref, lru_ba_ref, lru_bx_ref, lam_ref) = (
        r.at[0] for r in (mod_ref, gain_ref, w_in_ref, w_out_ref, conv_a_ref, sgu_w_ref,
                          sgu_b_ref, lru_cw_ref, lru_cb_ref, lru_w_ref, lru_ba_ref, lru_bx_ref,
                          lam_ref))
    tile = x_ref.shape[1]
    rows = ROW_CHUNK
    n_row_chunks = tile // rows
    n_sgu_chunks = tile // SGU_CHUNK
    step = pl.program_id(0)
    dyn_zero = step // pl.num_programs(0)
    b = step // seq_tiles
    b_next = jnp.minimum(step + 1, pl.num_programs(0) - 1) // seq_tiles

    @pl.when(step % seq_tiles == 0)
    def _():
        cx_carry[...] = jnp.zeros_like(cx_carry)
        rx_carry[...] = jnp.zeros_like(rx_carry)
        h_carry[...] = jnp.zeros_like(h_carry)

    if prep_next:
        for jb in range(N_COL_BLOCKS):
            for k in range(N_SLICES):
                c0 = k * D_MODEL + jb * LANES
                w_next_ref[jb, :, k * LANES:(k + 1) * LANES] = (
                    w_raw_ref[0, :, c0:c0 + LANES].astype(jnp.bfloat16))

    def residual_chunk(src_ref, c):
        xs = src_ref[0, c * rows:(c + 1) * rows, :]
        return _permute_chunk(perm_scr, c, xs, dyn_zero) if permute_in else xs

    def modulated_norm(src_ref, batch, dst_ref):
        shift = mod_ref[0, pl.ds(batch, 1), :]
        scale = mod_ref[1, pl.ds(batch, 1), :]
        gain_scale = gain_ref[...] * (1.0 + scale)
        for c in range(n_row_chunks):
            r0 = c * rows
            xs = residual_chunk(src_ref, c)
            inv = lax.rsqrt(jnp.mean(xs * xs, axis=-1, keepdims=True) + EPS)
            dst_ref[r0:r0 + rows, :] = ((xs * inv) * gain_scale + shift).astype(jnp.bfloat16)

    def project(j, operand_ref, half):
        cols = slice(0, BLOCK_COLS // 6) if half == 0 else slice(BLOCK_COLS // 6, BLOCK_COLS)
        p_scr[j % P_SLOTS, :, cols] = jnp.dot(operand_ref[...], w_in_ref[j, :, cols],
                                              preferred_element_type=jnp.float32)

    def position_of(row):
        return (row & -ROW_CHUNK) | ((row & (SUBLANES - 1)) * SUBLANES) | (
            (row // SUBLANES) & (SUBLANES - 1))

    @pl.when(step == 0)
    def _():
        modulated_norm(x_ref, b, h_scr)
        project(0, h_scr, 0)
        project(0, h_scr, 1)
        row = lax.broadcasted_iota(jnp.int32, (SGU_CHUNK, SGU_CHUNK), 0)
        col = lax.broadcasted_iota(jnp.int32, (SGU_CHUNK, SGU_CHUNK), 1)
        select_rows = (col == position_of(row)).astype(jnp.bfloat16)
        select_cols = (row == position_of(col)).astype(jnp.bfloat16)
        for head in range(SGU_HEADS):
            w = jnp.where(row >= col, sgu_w_ref[head], 0.0).astype(jnp.bfloat16)
            w = jnp.dot(w, select_cols, preferred_element_type=jnp.float32).astype(jnp.bfloat16)
            sgu_wp_scr[head] = jnp.dot(select_rows, w, preferred_element_type=jnp.float32
                                       ).astype(jnp.bfloat16)

    modulated_norm(x_next_ref, b_next, h_next_scr)
    gate = mod_ref[2, pl.ds(b, 1), :]

    def mix(j):
        col = j * LANES
        lanes = slice(col, col + LANES)
        p = p_scr.at[j % P_SLOTS]
        xc_s, xcb_s, vn_s, pre_s, z_s = (
            r.at[j % 2] for r in (xc_scr, xcb_scr, vn_scr, pre_scr, z_scr))

        def sl(k, r0):
            return p[r0:r0 + rows, k * LANES:(k + 1) * LANES]

        def sl3(k, r0):
            return sl(k, r0).reshape(SUBLANES, SUBLANES, LANES)

        for c in range(n_row_chunks):
            r0 = c * rows
            v = sl(S_V, r0)
            vc = v - jnp.mean(v, axis=-1, keepdims=True)
            var = jnp.mean(vc * vc, axis=-1, keepdims=True)
            vn_s[r0:r0 + rows, :] = (vc * lax.rsqrt(var + EPS)).astype(jnp.bfloat16)
        vn_wide = jnp.concatenate(
            [vn_s[n * SGU_CHUNK:(n + 1) * SGU_CHUNK, :] for n in range(n_sgu_chunks)], axis=1)
        z_s[...] = jnp.dot(sgu_wp_scr[j], vn_wide, preferred_element_type=jnp.float32)

        rx_tail = rx_carry[:, :, lanes]
        for c in range(n_row_chunks):
            r0 = c * rows
            xcp, rx_tail = _causal_conv_permuted(sl3(R_X, r0), rx_tail, lru_cw_ref, col)
            xc = (xcp + lru_cb_ref[:, lanes]).reshape(rows, LANES)
            xc_s[r0:r0 + rows, :] = xc
            xcb_s[r0:r0 + rows, :] = xc.astype(jnp.bfloat16)
        rx_carry[:, :, lanes] = rx_tail
        pre_s[...] = jnp.dot(xcb_s[...], lru_w_ref[j], preferred_element_type=jnp.float32)
        yield

        lam = lam_ref[:, lanes]
        softplus_neg_lam = jnp.maximum(-lam, 0.0) + jnp.log1p(jnp.exp(-jnp.abs(lam)))
        log2_a_per_r = (-LRU_C * LOG2_E) * softplus_neg_lam
        cx_tail = cx_carry[:, :, lanes]
        state = h_carry[:, lanes]
        for c in range(n_row_chunks):
            r0 = c * rows
            chunk, q0 = divmod(r0, SGU_CHUNK)

            convp, cx_tail = _causal_conv_permuted(sl3(A_C, r0) * sl3(A_X, r0), cx_tail,
                                                   conv_a_ref, col)
            conv = convp.reshape(rows, LANES)
            merged = _sigmoid(sl(G_A, r0)) * ((_silu(sl(A_Z, r0)) * sl(A_B, r0)) * conv)

            z = (z_s[q0:q0 + rows, chunk * LANES:(chunk + 1) * LANES]
                 + sgu_b_ref[j, q0:q0 + rows, :])
            merged = merged + _sigmoid(sl(G_S, r0)) * ((_silu(sl(S_Z, r0)) * sl(S_U, r0)) * z)

            r =_sigmoid(pre_s[r0:r0 + rows, :LANES] + lru_ba_ref[:, lanes])
            i = _sigmoid(pre_s[r0:r0 + rows, LANES:] + lru_bx_ref[:, lanes])
            a = jnp.exp2(r * log2_a_per_r)
            y = 1.0 - a * a
            bb = (y * lax.rsqrt(jnp.maximum(y, 1e-30))) * (i * xc_s[r0:r0 + rows, :])
            hsp, state = _linear_scan_permuted(a.reshape(SUBLANES, SUBLANES, LANES),
                                               bb.reshape(SUBLANES, SUBLANES, LANES), state)
            merged = merged + _sigmoid(sl(G_R, r0)) * (
                _silu(sl(R_Z, r0)) * hsp.reshape(rows, LANES))

            m_scr[r0:r0 + rows, lanes] = merged.astype(jnp.bfloat16)
        cx_carry[:, :, lanes] = cx_tail
        h_carry[:, lanes] = state

    for j in range(N_COL_BLOCKS):
        upcoming = (j + 1, h_scr) if j + 1 < N_COL_BLOCKS else (0, h_next_scr)
        mix_parts = mix(j)
        project(*upcoming, 0)
        next(mix_parts)
        project(*upcoming, 1)
        next(mix_parts, None)

    res = p_scr.at[(N_COL_BLOCKS - 1) % P_SLOTS]
    res[:, :D_MODEL] = jnp.dot(m_scr[...], w_out_ref[...], preferred_element_type=jnp.float32)
    for c in range(n_row_chunks):
        r0 = c * rows
        out = residual_chunk(x_ref, c) + gate * res[r0:r0 + rows, :D_MODEL]
        if final_norm:
            inv_o = lax.rsqrt(jnp.mean(out * out, axis=-1, keepdims=True) + EPS)
            out = (out * inv_o) * fgain_ref[...]
        if unpermute_out:
            out = _permute_chunk(perm_scr, c, out, dyn_zero)
        o_ref[0, r0:r0 + rows, :] = out
    h_scr[...] = h_next_scr[...]


def _layer_resident(arr, layer):
    index = (layer,) + (0,) * (arr.ndim - 1)
    return pl.BlockSpec((1,) + arr.shape[1:], lambda i: index, pipeline_mode=pl.Buffered(1))


def _tile_index(i, *, seq_tiles, last, ahead):
    t = jnp.minimum(i + ahead, last)
    return (t // seq_tiles, t % seq_tiles, 0)


def _layer(x, layer, stacked, w_in_blocked, w_in_raw, final_gain, *, last):
    bsz, seq, d = x.shape
    tile = SEQ_TILE
    seq_tiles = seq // tile
    n_steps = bsz * seq_tiles
    prep_next = layer + 1 < w_in_raw.shape[0]
    slab = d // n_steps
    tile_spec = functools.partial(_tile_index, seq_tiles=seq_tiles, last=n_steps - 1)
    operands =[(arr, layer) for arr in stacked]
    operands.insert(2, (w_in_blocked[None], 0))
    in_specs = [pl.BlockSpec((1, tile, d), functools.partial(tile_spec, ahead=0)),
                pl.BlockSpec((1, tile, d), functools.partial(tile_spec, ahead=1))]
    inputs = [x, x]
    out_specs = [pl.BlockSpec((1, tile, d), functools.partial(tile_spec, ahead=0))]
    out_shape = [jax.ShapeDtypeStruct(x.shape, x.dtype)]
    if prep_next:
        in_specs.append(pl.BlockSpec((1, slab, w_in_raw.shape[2]), lambda i: (layer + 1, i, 0)))
        inputs.append(w_in_raw)
        out_specs.append(pl.BlockSpec((N_COL_BLOCKS, slab, BLOCK_COLS), lambda i: (0, i, 0)))
        out_shape.append(jax.ShapeDtypeStruct((N_COL_BLOCKS, d, BLOCK_COLS), jnp.bfloat16))
    in_specs += [_layer_resident(arr, index) for arr, index in operands]
    in_specs.append(pl.BlockSpec(final_gain.shape, lambda i: (0, 0), pipeline_mode=pl.Buffered(1)))
    inputs += [arr for arr, _ in operands] + [final_gain]
    outs = pl.pallas_call(
        functools.partial(_layer_kernel, final_norm=last, seq_tiles=seq_tiles,
                          prep_next=prep_next, permute_in=(layer == 0), unpermute_out=last),
        grid=(n_steps,),
        in_specs=in_specs,
        out_specs=out_specs,
        out_shape=out_shape,
        scratch_shapes=[
            pltpu.VMEM((tile, d), jnp.bfloat16),
            pltpu.VMEM((tile, d), jnp.bfloat16),
            pltpu.VMEM((tile, d), jnp.bfloat16),
            pltpu.VMEM((P_SLOTS, tile, BLOCK_COLS), jnp.float32),
            pltpu.VMEM((2, tile, LANES), jnp.float32),
            pltpu.VMEM((2, tile, LANES), jnp.bfloat16),
            pltpu.VMEM((2, tile, LANES), jnp.bfloat16),
            pltpu.VMEM((2, tile, 2 * LANES), jnp.float32),
            pltpu.VMEM((2, SGU_CHUNK, tile), jnp.float32),
            pltpu.VMEM((SGU_HEADS, SGU_CHUNK, SGU_CHUNK), jnp.bfloat16),
            pltpu.VMEM((2, tile // ROW_CHUNK * PERM_ROWS, LANES), jnp.float32),
            pltpu.VMEM((CONV_K - 1, SUBLANES, d), jnp.float32),
            pltpu.VMEM((LRU_CONV_K - 1, SUBLANES, d), jnp.float32),
            pltpu.VMEM((1, d), jnp.float32),
        ],
        compiler_params=pltpu.CompilerParams(
            dimension_semantics=("arbitrary",),
            vmem_limit_bytes=VMEM_LIMIT_BYTES),
        name="hybrid_mixer_layer",
    )(*inputs)
    return (outs[0], outs[1]) if prep_next else (outs[0], None)


def _block_columns_kernel(*refs):
    o_ref = refs[-1]
    for k in range(N_SLICES):
        o_ref[0, :, k * LANES:(k + 1) * LANES] = refs[k][0].astype(jnp.bfloat16)


def _slice_block_index(j, *, layer, k):
    return (layer, 0, k * N_COL_BLOCKS + j)


def _block_columns(w_in, layer):
    _, d, _ = w_in.shape
    return pl.pallas_call(
        _block_columns_kernel,
        grid=(N_COL_BLOCKS,),
        in_specs=[pl.BlockSpec((1, d, LANES), functools.partial(_slice_block_index, layer=layer, k=k))
                  for k in range(N_SLICES)],
        out_specs=pl.BlockSpec((1, d, BLOCK_COLS), lambda j: (j, 0, 0)),
        out_shape=jax.ShapeDtypeStruct((N_COL_BLOCKS, d, BLOCK_COLS), jnp.bfloat16),
        name="block_columns_bf16",
    )(*([w_in] * N_SLICES))


def _permute_positions(arr, axis):
    axis = axis % arr.ndim
    shape = arr.shape
    split = arr.reshape(shape[:axis] + (SGU_CHUNK // ROW_CHUNK, SUBLANES, SUBLANES) + shape[axis + 1:])
    return jnp.swapaxes(split, axis + 1, axis + 2).reshape(shape)


def _pair_block_diag(w_a, w_x):
    def pairs(w):
        w = w.reshape(w.shape[0], LRU_HEADS // 2, 2, LRU_HEAD_DIM, LRU_HEAD_DIM)
        zero = jnp.zeros_like(w[:, :, 0])
        top = jnp.concatenate([w[:, :, 0], zero], axis=-1)
        bottom = jnp.concatenate([zero, w[:, :, 1]], axis=-1)
        return jnp.concatenate([top, bottom], axis=-2)
    return jnp.concatenate([pairs(w_a), pairs(w_x)], axis=-1)


def kernel(x, c, norm_gain, w_mod, b_mod, w_in, w_out, conv_a_w, sgu_w, sgu_b, lru_conv_w,
           lru_conv_b, lru_wa, lru_ba, lru_wx, lru_bx, lru_lambda, final_gain):
    depth = w_in.shape[0]
    d = x.shape[-1]
    stacked = (
        _modulation(c, w_mod, b_mod),
        norm_gain.reshape(depth, 1, d),
        w_out.astype(jnp.bfloat16),
        conv_a_w,
        sgu_w,
        jnp.broadcast_to(_permute_positions(sgu_b, -1)[..., None], sgu_b.shape + (LANES,)),
        lru_conv_w,
        lru_conv_b.reshape(depth, 1, d),
        _pair_block_diag(lru_wa, lru_wx).astype(jnp.bfloat16),
        lru_ba.reshape(depth, 1, d),
        lru_bx.reshape(depth, 1, d),
        lru_lambda.reshape(depth, 1, d),
    )
    w_in_blocked = _block_columns(w_in, 0)
    for l in range(depth):
        x, w_in_blocked = _layer(x, l, stacked, w_in_blocked, w_in, final_gain.reshape(1, d),
                                 last=(l == depth - 1))
    return x
```

```python
import functools
import math

import jax
import jax.numpy as jnp
from jax import lax
from jax.experimental import pallas as pl
from jax.experimental.pallas import tpu as pltpu

D_MODEL = 1024
SGU_CHUNK = 128
SGU_HEADS = 8
LRU_HEADS = 16
LRU_HEAD_DIM = 64
CONV_K = 3
LRU_CONV_K = 4
LRU_C = 8.0
EPS = 1e-6
LOG2_E = math.log2(math.e)

SUBLANES = 8
LANES = 128
N_SLICES = 12
(A_X, A_B, A_C, A_Z, S_U, S_V, S_Z, R_X, R_Z, G_A, G_S, G_R) = range(N_SLICES)
N_COL_BLOCKS = D_MODEL // LANES
BLOCK_COLS = N_SLICES * LANES
SEQ_TILE = 512
P_SLOTS = 2
ROW_CHUNK = SUBLANES * SUBLANES
PERM_PITCH = SUBLANES + 1
PERM_ROWS = SUBLANES * PERM_PITCH
VMEM_LIMIT_BYTES = 60 * 1024 * 1024


def _sigmoid(v):
    return 1.0 / (1.0 + jnp.exp2(v * (-LOG2_E)))


def _silu(v):
    return v * _sigmoid(v)


def _mod_kernel(c_ref, w_ref, b_ref, o_ref):
    c = c_ref[...]
    o_ref[0, 0] = jnp.dot(_silu(c), w_ref[0], preferred_element_type=jnp.float32) + b_ref[0, 0]


def _modulation(c, w_mod, b_mod):
    depth, d, _ = w_mod.shape
    bsz = c.shape[0]
    return pl.pallas_call(
        _mod_kernel,
        grid=(depth, 3),
        in_specs=[
            pl.BlockSpec((bsz, d), lambda l, k: (0, 0)),
            pl.BlockSpec((1, d, d), lambda l, k: (l, 0, k)),
            pl.BlockSpec((1, 1, 1, d), lambda l, k: (l, k, 0, 0)),
        ],
        out_specs=pl.BlockSpec((1, 1, bsz, d), lambda l, k: (l, k, 0, 0)),
        out_shape=jax.ShapeDtypeStruct((depth, 3, bsz, d), jnp.float32),
        name="adaln_modulation",
    )(c, w_mod, b_mod.reshape(depth, 3, 1, d))


def _permute_rows(scr, chunk, v3, dyn_zero):
    base = chunk * PERM_ROWS
    for g in range(SUBLANES):
        scr[pl.ds(base + g, SUBLANES, stride=PERM_PITCH), :] = v3[g]
    return jnp.stack([scr[pl.ds(dyn_zero + (base + PERM_PITCH * r), SUBLANES), :]
                      for r in range(SUBLANES)], axis=0)


def _permute_chunk(scr, chunk, xs, dyn_zero):
    cols = []
    for l in range(xs.shape[1] // LANES):
        v3 = xs[:, l * LANES:(l + 1) * LANES].reshape(SUBLANES, SUBLANES, LANES)
        plane = scr.at[l % scr.shape[0]]
        cols.append(_permute_rows(plane, chunk, v3, dyn_zero).reshape(ROW_CHUNK, LANES))
    return jnp.concatenate(cols, axis=1)


def _causal_conv_permuted(vp3, prev_tail, w_ref, col):
    k_taps = w_ref.shape[0]
    n_prev = k_taps - 1
    tail = vp3[SUBLANES - n_prev:]
    sublane = lax.broadcasted_iota(jnp.int32, tail.shape, 1)
    before = jnp.where(sublane >= 1, pltpu.roll(tail, 1, axis=1), pltpu.roll(prev_tail, 1, axis=1))
    ext = jnp.concatenate([before, vp3], axis=0)
    acc = w_ref[k_taps - 1:k_taps, col:col + LANES] * vp3
    for j in range(n_prev):
        acc = acc + w_ref[j:j + 1, col:col + LANES] * ext[j:j + SUBLANES]
    return acc, tail


def _linear_scan_permuted(ap3, bp3, h):
    a_cum = [ap3[0]]
    b_cum = [bp3[0]]
    for r in range(1, SUBLANES):
        b_cum.append(ap3[r] * b_cum[-1] + bp3[r])
        a_cum.append(ap3[r] * a_cum[-1])
    alpha, beta = a_cum[-1], b_cum[-1]
    sublane = lax.broadcasted_iota(jnp.int32, alpha.shape, 0)
    for k in (1, 2, 4):
        keep = sublane >= k
        alpha_prev = jnp.where(keep, pltpu.roll(alpha, k, axis=0), 1.0)
        beta_prev = jnp.where(keep, pltpu.roll(beta, k, axis=0), 0.0)
        beta = alpha * beta_prev + beta
        alpha = alpha * alpha_prev
    after = alpha * h + beta
    entering = jnp.where(sublane >= 1, pltpu.roll(after, 1, axis=0), h)
    states = [a_cum[r] * entering + b_cum[r] for r in range(SUBLANES)]
    return jnp.stack(states, axis=0), after[SUBLANES - 1:SUBLANES, :]


def _layer_kernel(*refs, final_norm, seq_tiles, prep_next, permute_in, unpermute_out):
    refs = list(refs)
    w_raw_ref = refs.pop(2) if prep_next else None
    w_next_ref = refs.pop(17) if prep_next else None
    (x_ref, x_next_ref, mod_ref, gain_ref, w_in_ref, w_out_ref, conv_a_ref, sgu_w_ref,
     sgu_b_ref, lru_cw_ref, lru_cb_ref, lru_w_ref, lru_ba_ref, lru_bx_ref,
     lam_ref, fgain_ref, o_ref,
     h_scr, h_next_scr, m_scr, p_scr, xc_scr, xcb_scr, vn_scr, pre_scr, z_scr, sgu_wp_scr,
     perm_scr, cx_carry, rx_carry, h_carry) = refs
    (mod_ref, gain_ref, w_in_ref, w_out_ref, conv_a_ref, sgu_w_ref, sgu_b_ref, lru_cw_ref,
     lru_cb_ref, lru_w_ref, lru_ba_ref, lru_bx_ref, lam_ref) = (
        r.at[0] for r in (mod_ref, gain_ref, w_in_ref, w_out_ref, conv_a_ref, sgu_w_ref,
                          sgu_b_ref, lru_cw_ref, lru_cb_ref, lru_w_ref, lru_ba_ref, lru_bx_ref,
                          lam_ref))
    tile = x_ref.shape[1]
    rows = ROW_CHUNK
    n_row_chunks = tile // rows
    n_sgu_chunks = tile // SGU_CHUNK
    step = pl.program_id(0)
    dyn_zero = step // pl.num_programs(0)
    b = step // seq_tiles
    b_next = jnp.minimum(step + 1, pl.num_programs(0) - 1) // seq_tiles

    @pl.when(step % seq_tiles == 0)
    def _():
        cx_carry[...] = jnp.zeros_like(cx_carry)
        rx_carry[...] = jnp.zeros_like(rx_carry)
        h_carry[...] = jnp.zeros_like(h_carry)

    if prep_next:
        for jb in range(N_COL_BLOCKS):
            for k in range(N_SLICES):
                c0 = k * D_MODEL + jb * LANES
                w_next_ref[jb, :, k * LANES:(k + 1) * LANES] = (
                    w_raw_ref[0, :, c0:c0 + LANES].astype(jnp.bfloat16))

    def residual_chunk(src_ref, c):
        xs = src_ref[0, c * rows:(c + 1) * rows, :]
        return _permute_chunk(perm_scr, c, xs, dyn_zero) if permute_in else xs

    def modulated_norm(src_ref, batch, dst_ref):
        shift = mod_ref[0, pl.ds(batch, 1), :]
        scale = mod_ref[1, pl.ds(batch, 1), :]
        gain_scale = gain_ref[...] * (1.0 + scale)
        for c in range(n_row_chunks):
            r0 = c * rows
            xs = residual_chunk(src_ref, c)
            inv = lax.rsqrt(jnp.mean(xs * xs, axis=-1, keepdims=True) + EPS)
            dst_ref[r0:r0 + rows, :] = ((xs * inv) * gain_scale + shift).astype(jnp.bfloat16)

    def project(j, operand_ref, half):
        cols = slice(0, BLOCK_COLS // 6) if half == 0 else slice(BLOCK_COLS // 6, BLOCK_COLS)
        p_scr[j % P_SLOTS, :, cols] = jnp.dot(operand_ref[...], w_in_ref[j, :, cols],
                                              preferred_element_type=jnp.float32)

    def position_of(row):
        return (row & -ROW_CHUNK) | ((row & (SUBLANES - 1)) * SUBLANES) | (
            (row // SUBLANES) & (SUBLANES - 1))

    @pl.when(step == 0)
    def _():
        modulated_norm(x_ref, b, h_scr)
        project(0, h_scr, 0)
        project(0, h_scr, 1)
        row = lax.broadcasted_iota(jnp.int32, (SGU_CHUNK, SGU_CHUNK), 0)
        col = lax.broadcasted_iota(jnp.int32, (SGU_CHUNK, SGU_CHUNK), 1)
        select_rows = (col == position_of(row)).astype(jnp.bfloat16)
        select_cols = (row == position_of(col)).astype(jnp.bfloat16)
        for head in range(SGU_HEADS):
            w = jnp.where(row >= col, sgu_w_ref[head], 0.0).astype(jnp.bfloat16)
            w = jnp.dot(w, select_cols, preferred_element_type=jnp.float32).astype(jnp.bfloat16)
            sgu_wp_scr[head] = jnp.dot(select_rows, w, preferred_element_type=jnp.float32
                                       ).astype(jnp.bfloat16)

    modulated_norm(x_next_ref, b_next, h_next_scr)
    gate = mod_ref[2, pl.ds(b, 1), :]

    def mix(j):
        col = j * LANES
        lanes = slice(col, col + LANES)
        p = p_scr.at[j % P_SLOTS]
        xc_s, xcb_s, vn_s, pre_s, z_s = (
            r.at[j % 2] for r in (xc_scr, xcb_scr, vn_scr, pre_scr, z_scr))

        def sl(k, r0):
            return p[r0:r0 + rows, k * LANES:(k + 1) * LANES]

        def sl3(k, r0):
            return sl(k, r0).reshape(SUBLANES, SUBLANES, LANES)

        rx_tail = rx_carry[:, :, lanes]
        for c in range(n_row_chunks):
            r0 = c * rows
            xcp, rx_tail = _causal_conv_permuted(sl3(R_X, r0), rx_tail, lru_cw_ref, col)
            xc = (xcp + lru_cb_ref[:, lanes]).reshape(rows, LANES)
            xc_s[r0:r0 + rows, :] = xc
            xcb_s[r0:r0 + rows, :] = xc.astype(jnp.bfloat16)
        rx_carry[:, :, lanes] = rx_tail
        pre_s[...] = jnp.dot(xcb_s[...], lru_w_ref[j], preferred_element_type=jnp.float32)

        for c in range(n_row_chunks):
            r0 = c * rows
            v = sl(S_V, r0)
            vc = v - jnp.mean(v, axis=-1, keepdims=True)
            var = jnp.mean(vc * vc, axis=-1, keepdims=True)
            vn_s[r0:r0 + rows, :] = (vc * lax.rsqrt(var + EPS)).astype(jnp.bfloat16)
        vn_wide = jnp.concatenate(
            [vn_s[n * SGU_CHUNK:(n + 1) * SGU_CHUNK, :] for n in range(n_sgu_chunks)], axis=1)
        z_s[...] = jnp.dot(sgu_wp_scr[j], vn_wide, preferred_element_type=jnp.float32)
        yield

        lam = lam_ref[:, lanes]
        softplus_neg_lam = jnp.maximum(-lam, 0.0) + jnp.log1p(jnp.exp(-jnp.abs(lam)))
        log2_a_per_r = (-LRU_C * LOG2_E) * softplus_neg_lam
        cx_tail = cx_carry[:, :, lanes]
        state = h_carry[:, lanes]
        for c in range(n_row_chunks):
            r0 = c * rows
            chunk, q0 = divmod(r0, SGU_CHUNK)

            convp, cx_tail = _causal_conv_permuted(sl3(A_C, r0) * sl3(A_X, r0), cx_tail,
                                                   conv_a_ref, col)
            conv = convp.reshape(rows, LANES)
            merged = _sigmoid(sl(G_A, r0)) * ((_silu(sl(A_Z, r0)) * sl(A_B, r0)) * conv)

            z = (z_s[q0:q0 + rows, chunk * LANES:(chunk + 1) * LANES]
                 + sgu_b_ref[j, q0:q0 + rows, :])
            merged = merged + _sigmoid(sl(G_S, r0)) * ((_silu(sl(S_Z, r0)) * sl(S_U, r0)) * z)

            r =_sigmoid(pre_s[r0:r0 + rows, :LANES] + lru_ba_ref[:, lanes])
            i = _sigmoid(pre_s[r0:r0 + rows, LANES:] + lru_bx_ref[:, lanes])
            a = jnp.exp2(r * log2_a_per_r)
            y = 1.0 - a * a
            bb = (y * lax.rsqrt(jnp.maximum(y, 1e-30))) * (i * xc_s[r0:r0 + rows, :])
            hsp, state = _linear_scan_permuted(a.reshape(SUBLANES, SUBLANES, LANES),
                                               bb.reshape(SUBLANES, SUBLANES, LANES), state)
            merged = merged + _sigmoid(sl(G_R, r0)) * (
                _silu(sl(R_Z, r0)) * hsp.reshape(rows, LANES))

            m_scr[r0:r0 + rows, lanes] = merged.astype(jnp.bfloat16)
        cx_carry[:, :, lanes] = cx_tail
        h_carry[:, lanes] = state

    for j in range(N_COL_BLOCKS):
        upcoming = (j + 1, h_scr) if j + 1 < N_COL_BLOCKS else (0, h_next_scr)
        mix_parts = mix(j)
        project(*upcoming, 0)
        next(mix_parts)
        project(*upcoming, 1)
        next(mix_parts, None)

    res = p_scr.at[(N_COL_BLOCKS - 1) % P_SLOTS]
    res[:, :D_MODEL] = jnp.dot(m_scr[...], w_out_ref[...], preferred_element_type=jnp.float32)
    for c in range(n_row_chunks):
        r0 = c * rows
        out = residual_chunk(x_ref, c) + gate * res[r0:r0 + rows, :D_MODEL]
        if final_norm:
            inv_o = lax.rsqrt(jnp.mean(out * out, axis=-1, keepdims=True) + EPS)
            out = (out * inv_o) * fgain_ref[...]
        if unpermute_out:
            out = _permute_chunk(perm_scr, c, out, dyn_zero)
        o_ref[0, r0:r0 + rows, :] = out
    h_scr[...] = h_next_scr[...]


def _layer_resident(arr, layer):
    index = (layer,) + (0,) * (arr.ndim - 1)
    return pl.BlockSpec((1,) + arr.shape[1:], lambda i: index, pipeline_mode=pl.Buffered(1))


def _tile_index(i, *, seq_tiles, last, ahead):
    t = jnp.minimum(i + ahead, last)
    return (t // seq_tiles, t % seq_tiles, 0)


def _layer(x, layer, stacked, w_in_blocked, w_in_raw, final_gain, *, last):
    bsz, seq, d = x.shape
    tile = SEQ_TILE
    seq_tiles = seq // tile
    n_steps = bsz * seq_tiles
    prep_next = layer + 1 < w_in_raw.shape[0]
    slab = d // n_steps
    tile_spec = functools.partial(_tile_index, seq_tiles=seq_tiles, last=n_steps - 1)
    operands =[(arr, layer) for arr in stacked]
    operands.insert(2, (w_in_blocked[None], 0))
    in_specs = [pl.BlockSpec((1, tile, d), functools.partial(tile_spec, ahead=0)),
                pl.BlockSpec((1, tile, d), functools.partial(tile_spec, ahead=1))]
    inputs = [x, x]
    out_specs = [pl.BlockSpec((1, tile, d), functools.partial(tile_spec, ahead=0))]
    out_shape = [jax.ShapeDtypeStruct(x.shape, x.dtype)]
    if prep_next:
        in_specs.append(pl.BlockSpec((1, slab, w_in_raw.shape[2]), lambda i: (layer + 1, i, 0)))
        inputs.append(w_in_raw)
        out_specs.append(pl.BlockSpec((N_COL_BLOCKS, slab, BLOCK_COLS), lambda i: (0, i, 0)))
        out_shape.append(jax.ShapeDtypeStruct((N_COL_BLOCKS, d, BLOCK_COLS), jnp.bfloat16))
    in_specs += [_layer_resident(arr, index) for arr, index in operands]
    in_specs.append(pl.BlockSpec(final_gain.shape, lambda i: (0, 0), pipeline_mode=pl.Buffered(1)))
    inputs += [arr for arr, _ in operands] + [final_gain]
    outs = pl.pallas_call(
        functools.partial(_layer_kernel, final_norm=last, seq_tiles=seq_tiles,
                          prep_next=prep_next, permute_in=(layer == 0), unpermute_out=last),
        grid=(n_steps,),
        in_specs=in_specs,
        out_specs=out_specs,
        out_shape=out_shape,
        scratch_shapes=[
            pltpu.VMEM((tile, d), jnp.bfloat16),
            pltpu.VMEM((tile, d), jnp.bfloat16),
            pltpu.VMEM((tile, d), jnp.bfloat16),
            pltpu.VMEM((P_SLOTS, tile, BLOCK_COLS), jnp.float32),
            pltpu.VMEM((2, tile, LANES), jnp.float32),
            pltpu.VMEM((2, tile, LANES), jnp.bfloat16),
            pltpu.VMEM((2, tile, LANES), jnp.bfloat16),
            pltpu.VMEM((2, tile, 2 * LANES), jnp.float32),
            pltpu.VMEM((2, SGU_CHUNK, tile), jnp.float32),
            pltpu.VMEM((SGU_HEADS, SGU_CHUNK, SGU_CHUNK), jnp.bfloat16),
            pltpu.VMEM((2, tile // ROW_CHUNK * PERM_ROWS, LANES), jnp.float32),
            pltpu.VMEM((CONV_K - 1, SUBLANES, d), jnp.float32),
            pltpu.VMEM((LRU_CONV_K - 1, SUBLANES, d), jnp.float32),
            pltpu.VMEM((1, d), jnp.float32),
        ],
        compiler_params=pltpu.CompilerParams(
            dimension_semantics=("arbitrary",),
            vmem_limit_bytes=VMEM_LIMIT_BYTES),
        name="hybrid_mixer_layer",
    )(*inputs)
    return (outs[0], outs[1]) if prep_next else (outs[0], None)


def _block_columns_kernel(*refs):
    o_ref = refs[-1]
    for k in range(N_SLICES):
        o_ref[0, :, k * LANES:(k + 1) * LANES] = refs[k][0].astype(jnp.bfloat16)


def _slice_block_index(j, *, layer, k):
    return (layer, 0, k * N_COL_BLOCKS + j)


def _block_columns(w_in, layer):
    _, d, _ = w_in.shape
    return pl.pallas_call(
        _block_columns_kernel,
        grid=(N_COL_BLOCKS,),
        in_specs=[pl.BlockSpec((1, d, LANES), functools.partial(_slice_block_index, layer=layer, k=k))
                  for k in range(N_SLICES)],
        out_specs=pl.BlockSpec((1, d, BLOCK_COLS), lambda j: (j, 0, 0)),
        out_shape=jax.ShapeDtypeStruct((N_COL_BLOCKS, d, BLOCK_COLS), jnp.bfloat16),
        name="block_columns_bf16",
    )(*([w_in] * N_SLICES))


def _permute_positions(arr, axis):
    axis = axis % arr.ndim
    shape = arr.shape
    split = arr.reshape(shape[:axis] + (SGU_CHUNK // ROW_CHUNK, SUBLANES, SUBLANES) + shape[axis + 1:])
    return jnp.swapaxes(split, axis + 1, axis + 2).reshape(shape)


def _pair_block_diag(w_a, w_x):
    def pairs(w):
        w = w.reshape(w.shape[0], LRU_HEADS // 2, 2, LRU_HEAD_DIM, LRU_HEAD_DIM)
        zero = jnp.zeros_like(w[:, :, 0])
        top = jnp.concatenate([w[:, :, 0], zero], axis=-1)
        bottom = jnp.concatenate([zero, w[:, :, 1]], axis=-1)
        return jnp.concatenate([top, bottom], axis=-2)
    return jnp.concatenate([pairs(w_a), pairs(w_x)], axis=-1)


def kernel(x, c, norm_gain, w_mod, b_mod, w_in, w_out, conv_a_w, sgu_w, sgu_b, lru_conv_w,
           lru_conv_b, lru_wa, lru_ba, lru_wx, lru_bx, lru_lambda, final_gain):
    depth = w_in.shape[0]
    d = x.shape[-1]
    stacked = (
        _modulation(c, w_mod, b_mod),
        norm_gain.reshape(depth, 1, d),
        w_out.astype(jnp.bfloat16),
        conv_a_w,
        sgu_w,
        jnp.broadcast_to(_permute_positions(sgu_b, -1)[..., None], sgu_b.shape + (LANES,)),
        lru_conv_w,
        lru_conv_b.reshape(depth, 1, d),
        _pair_block_diag(lru_wa, lru_wx).astype(jnp.bfloat16),
        lru_ba.reshape(depth, 1, d),
        lru_bx.reshape(depth, 1, d),
        lru_lambda.reshape(depth, 1, d),
    )
    w_in_blocked = _block_columns(w_in, 0)
    for l in range(depth):
        x, w_in_blocked = _layer(x, l, stacked, w_in_blocked, w_in, final_gain.reshape(1, d),
                                 last=(l == depth - 1))
    return x
```

```python
import functools
import math

import jax
import jax.numpy as jnp
from jax import lax
from jax.experimental import pallas as pl
from jax.experimental.pallas import tpu as pltpu

D_MODEL = 1024
SGU_CHUNK = 128
SGU_HEADS = 8
LRU_HEADS = 16
LRU_HEAD_DIM = 64
CONV_K = 3
LRU_CONV_K = 4
LRU_C = 8.0
EPS = 1e-6
LOG2_E = math.log2(math.e)

SUBLANES = 8
LANES = 128
N_SLICES = 12
(A_X, A_B, A_C, A_Z, S_U, S_V, S_Z, R_X, R_Z, G_A, G_S, G_R) = range(N_SLICES)
N_COL_BLOCKS = D_MODEL // LANES
BLOCK_COLS = N_SLICES * LANES
SEQ_TILE = 512
P_SLOTS = 2
ROW_CHUNK = SUBLANES * SUBLANES
PERM_PITCH = SUBLANES + 1
PERM_ROWS = SUBLANES * PERM_PITCH
VMEM_LIMIT_BYTES = 60 * 1024 * 1024


def _sigmoid(v):
    return 1.0 / (1.0 + jnp.exp2(v * (-LOG2_E)))


def _silu(v):
    return v * _sigmoid(v)


def _mod_kernel(c_ref, w_ref, b_ref, o_ref):
    c = c_ref[...]
    o_ref[0, 0] = jnp.dot(_silu(c), w_ref[0], preferred_element_type=jnp.float32) + b_ref[0, 0]


def _modulation(c, w_mod, b_mod):
    depth, d, _ = w_mod.shape
    bsz = c.shape[0]
    return pl.pallas_call(
        _mod_kernel,
        grid=(depth, 3),
        in_specs=[
            pl.BlockSpec((bsz, d), lambda l, k: (0, 0)),
            pl.BlockSpec((1, d, d), lambda l, k: (l, 0, k)),
            pl.BlockSpec((1, 1, 1, d), lambda l, k: (l, k, 0, 0)),
        ],
        out_specs=pl.BlockSpec((1, 1, bsz, d), lambda l, k: (l, k, 0, 0)),
        out_shape=jax.ShapeDtypeStruct((depth, 3, bsz, d), jnp.float32),
        name="adaln_modulation",
    )(c, w_mod, b_mod.reshape(depth, 3, 1, d))


def _permute_rows(scr, chunk, v3, dyn_zero):
    base = chunk * PERM_ROWS
    for g in range(SUBLANES):
        scr[pl.ds(base + g, SUBLANES, stride=PERM_PITCH), :] = v3[g]
    return jnp.stack([scr[pl.ds(dyn_zero + (base + PERM_PITCH * r), SUBLANES), :]
                      for r in range(SUBLANES)], axis=0)


def _permute_chunk(scr, chunk, xs, dyn_zero):
    cols = []
    for l in range(xs.shape[1] // LANES):
        v3 = xs[:, l * LANES:(l + 1) * LANES].reshape(SUBLANES, SUBLANES, LANES)
        plane = scr.at[l % scr.shape[0]]
        cols.append(_permute_rows(plane, chunk, v3, dyn_zero).reshape(ROW_CHUNK, LANES))
    return jnp.concatenate(cols, axis=1)


def _causal_conv_permuted(vp3, prev_tail, w_ref, col):
    k_taps = w_ref.shape[0]
    n_prev = k_taps - 1
    tail = vp3[SUBLANES - n_prev:]
    sublane = lax.broadcasted_iota(jnp.int32, tail.shape, 1)
    before = jnp.where(sublane >= 1, pltpu.roll(tail, 1, axis=1), pltpu.roll(prev_tail, 1, axis=1))
    ext = jnp.concatenate([before, vp3], axis=0)
    acc = w_ref[k_taps - 1:k_taps, col:col + LANES] * vp3
    for j in range(n_prev):
        acc = acc + w_ref[j:j + 1, col:col + LANES] * ext[j:j + SUBLANES]
    return acc, tail


def _linear_scan_permuted(ap3, bp3, h):
    a_cum = [ap3[0]]
    b_cum = [bp3[0]]
    for r in range(1, SUBLANES):
        b_cum.append(ap3[r] * b_cum[-1] + bp3[r])
        a_cum.append(ap3[r] * a_cum[-1])
    alpha, beta = a_cum[-1], b_cum[-1]
    sublane = lax.broadcasted_iota(jnp.int32, alpha.shape, 0)
    for k in (1, 2, 4):
        keep = sublane >= k
        alpha_prev = jnp.where(keep, pltpu.roll(alpha, k, axis=0), 1.0)
        beta_prev = jnp.where(keep, pltpu.roll(beta, k, axis=0), 0.0)
        beta = alpha * beta_prev + beta
        alpha = alpha * alpha_prev
    after = alpha * h + beta
    entering = jnp.where(sublane >= 1, pltpu.roll(after, 1, axis=0), h)
    states = [a_cum[r] * entering + b_cum[r] for r in range(SUBLANES)]
    return jnp.stack(states, axis=0), after[SUBLANES - 1:SUBLANES, :]


def _layer_kernel(*refs, final_norm, seq_tiles, prep_next, permute_in, unpermute_out):
    refs = list(refs)
    w_raw_ref = refs.pop(2) if prep_next else None
    w_next_ref = refs.pop(17) if prep_next else None
    (x_ref, x_next_ref, mod_ref, gain_ref, w_in_ref, w_out_ref, conv_a_ref, sgu_w_ref,
     sgu_b_ref, lru_cw_ref, lru_cb_ref, lru_w_ref, lru_ba_ref, lru_bx_ref,
     lam_ref, fgain_ref, o_ref,
     h_scr, h_next_scr, m_scr, p_scr, xc_scr, xcb_scr, vn_scr, pre_scr, z_scr, sgu_wp_scr,
     perm_scr, cx_carry, rx_carry, h_carry) = refs
    (mod_ref, gain_ref, w_in_ref, w_out_ref, conv_a_ref, sgu_w_ref, sgu_b_ref, lru_cw_ref,
     lru_cb_ref, lru_w_ref, lru_ba_ref, lru_bx_ref, lam_ref) = (
        r.at[0] for r in (mod_ref, gain_ref, w_in_ref, w_out_ref, conv_a_ref, sgu_w_ref,
                          sgu_b_ref, lru_cw_ref, lru_cb_ref, lru_w_ref, lru_ba_ref, lru_bx_ref,
                          lam_ref))
    tile = x_ref.shape[1]
    rows = ROW_CHUNK
    n_row_chunks = tile // rows
    n_sgu_chunks = tile // SGU_CHUNK
    step = pl.program_id(0)
    dyn_zero = step // pl.num_programs(0)
    b = step // seq_tiles
    b_next = jnp.minimum(step + 1, pl.num_programs(0) - 1) // seq_tiles

    @pl.when(step % seq_tiles == 0)
    def _():
        cx_carry[...] = jnp.zeros_like(cx_carry)
        rx_carry[...] = jnp.zeros_like(rx_carry)
        h_carry[...] = jnp.zeros_like(h_carry)

    if prep_next:
        for jb in range(N_COL_BLOCKS):
            for k in range(N_SLICES):
                c0 = k * D_MODEL + jb * LANES
                w_next_ref[jb, :, k * LANES:(k + 1) * LANES] = (
                    w_raw_ref[0, :, c0:c0 + LANES].astype(jnp.bfloat16))

    def residual_chunk(src_ref, c):
        xs = src_ref[0, c * rows:(c + 1) * rows, :]
        return _permute_chunk(perm_scr, c, xs, dyn_zero) if permute_in else xs

    def modulated_norm(src_ref, batch, dst_ref):
        shift = mod_ref[0, pl.ds(batch, 1), :]
        scale = mod_ref[1, pl.ds(batch, 1), :]
        gain_scale = gain_ref[...] * (1.0 + scale)
        for c in range(n_row_chunks):
            r0 = c * rows
            xs = residual_chunk(src_ref, c)
            inv = lax.rsqrt(jnp.mean(xs * xs, axis=-1, keepdims=True) + EPS)
            dst_ref[r0:r0 + rows, :] = ((xs * inv) * gain_scale + shift).astype(jnp.bfloat16)

    def project(j, operand_ref, half):
        cols = slice(0, BLOCK_COLS // 6) if half == 0 else slice(BLOCK_COLS // 6, BLOCK_COLS)
        p_scr[j % P_SLOTS, :, cols] = jnp.dot(operand_ref[...], w_in_ref[j, :, cols],
                                              preferred_element_type=jnp.float32)

    def position_of(row):
        return (row & -ROW_CHUNK) | ((row & (SUBLANES - 1)) * SUBLANES) | (
            (row // SUBLANES) & (SUBLANES - 1))

    @pl.when(step == 0)
    def _():
        modulated_norm(x_ref, b, h_scr)
        project(0, h_scr, 0)
        project(0, h_scr, 1)
        row = lax.broadcasted_iota(jnp.int32, (SGU_CHUNK, SGU_CHUNK), 0)
        col = lax.broadcasted_iota(jnp.int32, (SGU_CHUNK, SGU_CHUNK), 1)
        select_rows = (col == position_of(row)).astype(jnp.bfloat16)
        select_cols = (row == position_of(col)).astype(jnp.bfloat16)
        for head in range(SGU_HEADS):
            w = jnp.where(row >= col, sgu_w_ref[head], 0.0).astype(jnp.bfloat16)
            w = jnp.dot(w, select_cols, preferred_element_type=jnp.float32).astype(jnp.bfloat16)
            sgu_wp_scr[head] = jnp.dot(select_rows, w, preferred_element_type=jnp.float32
                                       ).astype(jnp.bfloat16)

    modulated_norm(x_next_ref, b_next, h_next_scr)
    gate = mod_ref[2, pl.ds(b, 1), :]

    def mix(j):
        col = j * LANES
        lanes = slice(col, col + LANES)
        p = p_scr.at[j % P_SLOTS]
        xc_s, xcb_s, vn_s, pre_s, z_s = (
            r.at[j % 2] for r in (xc_scr, xcb_scr, vn_scr, pre_scr, z_scr))

        def sl(k, r0):
            return p[r0:r0 + rows, k * LANES:(k + 1) * LANES]

        def sl3(k, r0):
            return sl(k, r0).reshape(SUBLANES, SUBLANES, LANES)

        rx_tail = rx_carry[:, :, lanes]
        for c in range(n_row_chunks):
            r0 = c * rows
            xcp, rx_tail = _causal_conv_permuted(sl3(R_X, r0), rx_tail, lru_cw_ref, col)
            xc = (xcp + lru_cb_ref[:, lanes]).reshape(rows, LANES)
            xc_s[r0:r0 + rows, :] = xc
            xcb_s[r0:r0 + rows, :] = xc.astype(jnp.bfloat16)
            v = sl(S_V, r0)
            vc = v - jnp.mean(v, axis=-1, keepdims=True)
            var = jnp.mean(vc * vc, axis=-1, keepdims=True)
            vn_s[r0:r0 + rows, :] = (vc * lax.rsqrt(var + EPS)).astype(jnp.bfloat16)
        rx_carry[:, :, lanes] = rx_tail
        yield

        pre_s[...] = jnp.dot(xcb_s[...], lru_w_ref[j], preferred_element_type=jnp.float32)
        vn_wide = jnp.concatenate(
            [vn_s[n * SGU_CHUNK:(n + 1) * SGU_CHUNK, :] for n in range(n_sgu_chunks)], axis=1)
        z_s[...] = jnp.dot(sgu_wp_scr[j], vn_wide, preferred_element_type=jnp.float32)
        yield

        lam = lam_ref[:, lanes]
        softplus_neg_lam = jnp.maximum(-lam, 0.0) + jnp.log1p(jnp.exp(-jnp.abs(lam)))
        log2_a_per_r = (-LRU_C * LOG2_E) * softplus_neg_lam
        cx_tail = cx_carry[:, :, lanes]
        state = h_carry[:, lanes]
        for c in range(n_row_chunks):
            r0 = c * rows
            chunk, q0 = divmod(r0, SGU_CHUNK)

            convp, cx_tail = _causal_conv_permuted(sl3(A_C, r0) * sl3(A_X, r0), cx_tail,
                                                   conv_a_ref, col)
            conv = convp.reshape(rows, LANES)
            merged = _sigmoid(sl(G_A, r0)) * ((_silu(sl(A_Z, r0)) * sl(A_B, r0)) * conv)

            z = (z_s[q0:q0 + rows, chunk * LANES:(chunk + 1) * LANES]
                 + sgu_b_ref[j, q0:q0 + rows, :])
            merged = merged + _sigmoid(sl(G_S, r0)) * ((_silu(sl(S_Z, r0)) * sl(S_U, r0)) * z)

            r =_sigmoid(pre_s[r0:r0 + rows, :LANES] + lru_ba_ref[:, lanes])
            i = _sigmoid(pre_s[r0:r0 + rows, LANES:] + lru_bx_ref[:, lanes])
            a = jnp.exp2(r * log2_a_per_r)
            y = 1.0 - a * a
            bb = (y * lax.rsqrt(jnp.maximum(y, 1e-30))) * (i * xc_s[r0:r0 + rows, :])
            hsp, state = _linear_scan_permuted(a.reshape(SUBLANES, SUBLANES, LANES),
                                               bb.reshape(SUBLANES, SUBLANES, LANES), state)
            merged = merged + _sigmoid(sl(G_R, r0)) * (
                _silu(sl(R_Z, r0)) * hsp.reshape(rows, LANES))

            m_scr[r0:r0 + rows, lanes] = merged.astype(jnp.bfloat16)
        cx_carry[:, :, lanes] = cx_tail
        h_carry[:, lanes] = state

    for j in range(N_COL_BLOCKS):
        upcoming = (j + 1, h_scr) if j + 1 < N_COL_BLOCKS else (0, h_next_scr)
        mix_parts = mix(j)
        next(mix_parts)
        project(*upcoming, 0)
        next(mix_parts)
        project(*upcoming, 1)
        next(mix_parts, None)

    res = p_scr.at[(N_COL_BLOCKS - 1) % P_SLOTS]
    res[:, :D_MODEL] = jnp.dot(m_scr[...], w_out_ref[...], preferred_element_type=jnp.float32)
    for c in range(n_row_chunks):
        r0 = c * rows
        out = residual_chunk(x_ref, c) + gate * res[r0:r0 + rows, :D_MODEL]
        if final_norm:
            inv_o = lax.rsqrt(jnp.mean(out * out, axis=-1, keepdims=True) + EPS)
            out = (out * inv_o) * fgain_ref[...]
        if unpermute_out:
            out = _permute_chunk(perm_scr, c, out, dyn_zero)
        o_ref[0, r0:r0 + rows, :] = out
    h_scr[...] = h_next_scr[...]


def _layer_resident(arr, layer):
    index = (layer,) + (0,) * (arr.ndim - 1)
    return pl.BlockSpec((1,) + arr.shape[1:], lambda i: index, pipeline_mode=pl.Buffered(1))


def _tile_index(i, *, seq_tiles, last, ahead):
    t = jnp.minimum(i + ahead, last)
    return (t // seq_tiles, t % seq_tiles, 0)


def _layer(x, layer, stacked, w_in_blocked, w_in_raw, final_gain, *, last):
    bsz, seq, d = x.shape
    tile = SEQ_TILE
    seq_tiles = seq // tile
    n_steps = bsz * seq_tiles
    prep_next = layer + 1 < w_in_raw.shape[0]
    slab = d // n_steps
    tile_spec = functools.partial(_tile_index, seq_tiles=seq_tiles, last=n_steps - 1)
    operands =[(arr, layer) for arr in stacked]
    operands.insert(2, (w_in_blocked[None], 0))
    in_specs = [pl.BlockSpec((1, tile, d), functools.partial(tile_spec, ahead=0)),
                pl.BlockSpec((1, tile, d), functools.partial(tile_spec, ahead=1))]
    inputs = [x, x]
    out_specs = [pl.BlockSpec((1, tile, d), functools.partial(tile_spec, ahead=0))]
    out_shape = [jax.ShapeDtypeStruct(x.shape, x.dtype)]
    if prep_next:
        in_specs.append(pl.BlockSpec((1, slab, w_in_raw.shape[2]), lambda i: (layer + 1, i, 0)))
        inputs.append(w_in_raw)
        out_specs.append(pl.BlockSpec((N_COL_BLOCKS, slab, BLOCK_COLS), lambda i: (0, i, 0)))
        out_shape.append(jax.ShapeDtypeStruct((N_COL_BLOCKS, d, BLOCK_COLS), jnp.bfloat16))
    in_specs += [_layer_resident(arr, index) for arr, index in operands]
    in_specs.append(pl.BlockSpec(final_gain.shape, lambda i: (0, 0), pipeline_mode=pl.Buffered(1)))
    inputs += [arr for arr, _ in operands] + [final_gain]
    outs = pl.pallas_call(
        functools.partial(_layer_kernel, final_norm=last, seq_tiles=seq_tiles,
                          prep_next=prep_next, permute_in=(layer == 0), unpermute_out=last),
        grid=(n_steps,),
        in_specs=in_specs,
        out_specs=out_specs,
        out_shape=out_shape,
        scratch_shapes=[
            pltpu.VMEM((tile, d), jnp.bfloat16),
            pltpu.VMEM((tile, d), jnp.bfloat16),
            pltpu.VMEM((tile, d), jnp.bfloat16),
            pltpu.VMEM((P_SLOTS, tile, BLOCK_COLS), jnp.float32),
            pltpu.VMEM((2, tile, LANES), jnp.float32),
            pltpu.VMEM((2, tile, LANES), jnp.bfloat16),
            pltpu.VMEM((2, tile, LANES), jnp.bfloat16),
            pltpu.VMEM((2, tile, 2 * LANES), jnp.float32),
            pltpu.VMEM((2, SGU_CHUNK, tile), jnp.float32),
            pltpu.VMEM((SGU_HEADS, SGU_CHUNK, SGU_CHUNK), jnp.bfloat16),
            pltpu.VMEM((2, tile // ROW_CHUNK * PERM_ROWS, LANES), jnp.float32),
            pltpu.VMEM((CONV_K - 1, SUBLANES, d), jnp.float32),
            pltpu.VMEM((LRU_CONV_K - 1, SUBLANES, d), jnp.float32),
            pltpu.VMEM((1, d), jnp.float32),
        ],
        compiler_params=pltpu.CompilerParams(
            dimension_semantics=("arbitrary",),
            vmem_limit_bytes=VMEM_LIMIT_BYTES),
        name="hybrid_mixer_layer",
    )(*inputs)
    return (outs[0], outs[1]) if prep_next else (outs[0], None)


def _block_columns_kernel(*refs):
    o_ref = refs[-1]
    for k in range(N_SLICES):
        o_ref[0, :, k * LANES:(k + 1) * LANES] = refs[k][0].astype(jnp.bfloat16)


def _slice_block_index(j, *, layer, k):
    return (layer, 0, k * N_COL_BLOCKS + j)


def _block_columns(w_in, layer):
    _, d, _ = w_in.shape
    return pl.pallas_call(
        _block_columns_kernel,
        grid=(N_COL_BLOCKS,),
        in_specs=[pl.BlockSpec((1, d, LANES), functools.partial(_slice_block_index, layer=layer, k=k))
                  for k in range(N_SLICES)],
        out_specs=pl.BlockSpec((1, d, BLOCK_COLS), lambda j: (j, 0, 0)),
        out_shape=jax.ShapeDtypeStruct((N_COL_BLOCKS, d, BLOCK_COLS), jnp.bfloat16),
        name="block_columns_bf16",
    )(*([w_in] * N_SLICES))


def _permute_positions(arr, axis):
    axis = axis % arr.ndim
    shape = arr.shape
    split = arr.reshape(shape[:axis] + (SGU_CHUNK // ROW_CHUNK, SUBLANES, SUBLANES) + shape[axis + 1:])
    return jnp.swapaxes(split, axis + 1, axis + 2).reshape(shape)


def _pair_block_diag(w_a, w_x):
    def pairs(w):
        w = w.reshape(w.shape[0], LRU_HEADS // 2, 2, LRU_HEAD_DIM, LRU_HEAD_DIM)
        zero = jnp.zeros_like(w[:, :, 0])
        top = jnp.concatenate([w[:, :, 0], zero], axis=-1)
        bottom = jnp.concatenate([zero, w[:, :, 1]], axis=-1)
        return jnp.concatenate([top, bottom], axis=-2)
    return jnp.concatenate([pairs(w_a), pairs(w_x)], axis=-1)


def kernel(x, c, norm_gain, w_mod, b_mod, w_in, w_out, conv_a_w, sgu_w, sgu_b, lru_conv_w,
           lru_conv_b, lru_wa, lru_ba, lru_wx, lru_bx, lru_lambda, final_gain):
    depth = w_in.shape[0]
    d = x.shape[-1]
    stacked = (
        _modulation(c, w_mod, b_mod),
        norm_gain.reshape(depth, 1, d),
        w_out.astype(jnp.bfloat16),
        conv_a_w,
        sgu_w,
        jnp.broadcast_to(_permute_positions(sgu_b, -1)[..., None], sgu_b.shape + (LANES,)),
        lru_conv_w,
        lru_conv_b.reshape(depth, 1, d),
        _pair_block_diag(lru_wa, lru_wx).astype(jnp.bfloat16),
        lru_ba.reshape(depth, 1, d),
        lru_bx.reshape(depth, 1, d),
        lru_lambda.reshape(depth, 1, d),
    )
    w_in_blocked = _block_columns(w_in, 0)
    for l in range(depth):
        x, w_in_blocked = _layer(x, l, stacked, w_in_blocked, w_in, final_gain.reshape(1, d),
                                 last=(l == depth - 1))
    return x
```

```python
import functools
import math

import jax
import jax.numpy as jnp
from jax import lax
from jax.experimental import pallas as pl
from jax.experimental.pallas import tpu as pltpu

D_MODEL = 1024
SGU_CHUNK = 128
SGU_HEADS = 8
LRU_HEADS = 16
LRU_HEAD_DIM = 64
CONV_K = 3
LRU_CONV_K = 4
LRU_C = 8.0
EPS = 1e-6
LOG2_E = math.log2(math.e)

SUBLANES = 8
LANES = 128
N_SLICES = 12
(A_X, A_B, A_C, A_Z, S_U, S_V, S_Z, R_X, R_Z, G_A, G_S, G_R) = range(N_SLICES)
N_COL_BLOCKS = D_MODEL // LANES
BLOCK_COLS = N_SLICES * LANES
SEQ_TILE = 512
P_SLOTS = 2
ROW_CHUNK = SUBLANES * SUBLANES
PERM_PITCH = SUBLANES + 1
PERM_ROWS = SUBLANES * PERM_PITCH
VMEM_LIMIT_BYTES = 60 * 1024 * 1024


def _sigmoid(v):
    return 1.0 / (1.0 + jnp.exp2(v * (-LOG2_E)))


def _silu(v):
    return v * _sigmoid(v)


def _mod_kernel(c_ref, w_ref, b_ref, o_ref):
    c = c_ref[...]
    o_ref[0, 0] = jnp.dot(_silu(c), w_ref[0], preferred_element_type=jnp.float32) + b_ref[0, 0]


def _modulation(c, w_mod, b_mod):
    depth, d, _ = w_mod.shape
    bsz = c.shape[0]
    return pl.pallas_call(
        _mod_kernel,
        grid=(depth, 3),
        in_specs=[
            pl.BlockSpec((bsz, d), lambda l, k: (0, 0)),
            pl.BlockSpec((1, d, d), lambda l, k: (l, 0, k)),
            pl.BlockSpec((1, 1, 1, d), lambda l, k: (l, k, 0, 0)),
        ],
        out_specs=pl.BlockSpec((1, 1, bsz, d), lambda l, k: (l, k, 0, 0)),
        out_shape=jax.ShapeDtypeStruct((depth, 3, bsz, d), jnp.float32),
        name="adaln_modulation",
    )(c, w_mod, b_mod.reshape(depth, 3, 1, d))


def _permute_rows(scr, chunk, v3, dyn_zero):
    base = chunk * PERM_ROWS
    for g in range(SUBLANES):
        scr[pl.ds(base + g, SUBLANES, stride=PERM_PITCH), :] = v3[g]
    return jnp.stack([scr[pl.ds(dyn_zero + (base + PERM_PITCH * r), SUBLANES), :]
                      for r in range(SUBLANES)], axis=0)


def _permute_chunk(scr, chunk, xs, dyn_zero):
    cols = []
    for l in range(xs.shape[1] // LANES):
        v3 = xs[:, l * LANES:(l + 1) * LANES].reshape(SUBLANES, SUBLANES, LANES)
        plane = scr.at[l % scr.shape[0]]
        cols.append(_permute_rows(plane, chunk, v3, dyn_zero).reshape(ROW_CHUNK, LANES))
    return jnp.concatenate(cols, axis=1)


def _causal_conv_permuted(vp3, prev_tail, w_ref, col):
    k_taps = w_ref.shape[0]
    n_prev = k_taps - 1
    tail = vp3[SUBLANES - n_prev:]
    sublane = lax.broadcasted_iota(jnp.int32, tail.shape, 1)
    before = jnp.where(sublane >= 1, pltpu.roll(tail, 1, axis=1), pltpu.roll(prev_tail, 1, axis=1))
    ext = jnp.concatenate([before, vp3], axis=0)
    acc = w_ref[k_taps - 1:k_taps, col:col + LANES] * vp3
    for j in range(n_prev):
        acc = acc + w_ref[j:j + 1, col:col + LANES] * ext[j:j + SUBLANES]
    return acc, tail


def _linear_scan_permuted(ap3, bp3, h):
    a_cum = [ap3[0]]
    b_cum = [bp3[0]]
    for r in range(1, SUBLANES):
        b_cum.append(ap3[r] * b_cum[-1] + bp3[r])
        a_cum.append(ap3[r] * a_cum[-1])
    alpha, beta = a_cum[-1], b_cum[-1]
    sublane = lax.broadcasted_iota(jnp.int32, alpha.shape, 0)
    for k in (1, 2, 4):
        keep = sublane >= k
        alpha_prev = jnp.where(keep, pltpu.roll(alpha, k, axis=0), 1.0)
        beta_prev = jnp.where(keep, pltpu.roll(beta, k, axis=0), 0.0)
        beta = alpha * beta_prev + beta
        alpha = alpha * alpha_prev
    after = alpha * h + beta
    entering = jnp.where(sublane >= 1, pltpu.roll(after, 1, axis=0), h)
    states = [a_cum[r] * entering + b_cum[r] for r in range(SUBLANES)]
    return jnp.stack(states, axis=0), after[SUBLANES - 1:SUBLANES, :]


def _layer_kernel(*refs, final_norm, seq_tiles, prep_next, permute_in, unpermute_out):
    refs = list(refs)
    w_raw_ref = refs.pop(2) if prep_next else None
    w_next_ref = refs.pop(17) if prep_next else None
    (x_ref, x_next_ref, mod_ref, gain_ref, w_in_ref, w_out_ref, conv_a_ref, sgu_w_ref,
     sgu_b_ref, lru_cw_ref, lru_cb_ref, lru_w_ref, lru_ba_ref, lru_bx_ref,
     lam_ref, fgain_ref, o_ref,
     h_scr, h_next_scr, m_scr, p_scr, xc_scr, xcb_scr, vn_scr, pre_scr, z_scr, sgu_wp_scr,
     perm_scr, cx_carry, rx_carry, h_carry) = refs
    (mod_ref, gain_ref, w_in_ref, w_out_ref, conv_a_ref, sgu_w_ref, sgu_b_ref, lru_cw_ref,
     lru_cb_ref, lru_w_ref, lru_ba_ref, lru_bx_ref, lam_ref) = (
        r.at[0] for r in (mod_ref, gain_ref, w_in_ref, w_out_ref, conv_a_ref, sgu_w_ref,
                          sgu_b_ref, lru_cw_ref, lru_cb_ref, lru_w_ref, lru_ba_ref, lru_bx_ref,
                          lam_ref))
    tile = x_ref.shape[1]
    rows = ROW_CHUNK
    n_row_chunks = tile // rows
    n_sgu_chunks = tile // SGU_CHUNK
    step = pl.program_id(0)
    dyn_zero = step // pl.num_programs(0)
    b = step // seq_tiles
    b_next = jnp.minimum(step + 1, pl.num_programs(0) - 1) // seq_tiles

    @pl.when(step % seq_tiles == 0)
    def _():
        cx_carry[...] = jnp.zeros_like(cx_carry)
        rx_carry[...] = jnp.zeros_like(rx_carry)
        h_carry[...] = jnp.zeros_like(h_carry)

    if prep_next:
        for jb in range(N_COL_BLOCKS):
            for k in range(N_SLICES):
                c0 = k * D_MODEL + jb * LANES
                w_next_ref[jb, :, k * LANES:(k + 1) * LANES] = (
                    w_raw_ref[0, :, c0:c0 + LANES].astype(jnp.bfloat16))

    def residual_chunk(src_ref, c):
        xs = src_ref[0, c * rows:(c + 1) * rows, :]
        return _permute_chunk(perm_scr, c, xs, dyn_zero) if permute_in else xs

    def modulated_norm(src_ref, batch, dst_ref):
        shift = mod_ref[0, pl.ds(batch, 1), :]
        scale = mod_ref[1, pl.ds(batch, 1), :]
        gain_scale = gain_ref[...] * (1.0 + scale)
        for c in range(n_row_chunks):
            r0 = c * rows
            xs = residual_chunk(src_ref, c)
            inv = lax.rsqrt(jnp.mean(xs * xs, axis=-1, keepdims=True) + EPS)
            dst_ref[r0:r0 + rows, :] = ((xs * inv) * gain_scale + shift).astype(jnp.bfloat16)

    def project(j, operand_ref, half):
        cols = slice(0, BLOCK_COLS // 6) if half == 0 else slice(BLOCK_COLS // 6, BLOCK_COLS)
        p_scr[j % P_SLOTS, :, cols] = jnp.dot(operand_ref[...], w_in_ref[j, :, cols],
                                              preferred_element_type=jnp.float32)

    def position_of(row):
        return (row & -ROW_CHUNK) | ((row & (SUBLANES - 1)) * SUBLANES) | (
            (row // SUBLANES) & (SUBLANES - 1))

    @pl.when(step == 0)
    def _():
        modulated_norm(x_ref, b, h_scr)
        project(0, h_scr, 0)
        project(0, h_scr, 1)
        row = lax.broadcasted_iota(jnp.int32, (SGU_CHUNK, SGU_CHUNK), 0)
        col = lax.broadcasted_iota(jnp.int32, (SGU_CHUNK, SGU_CHUNK), 1)
        select_rows = (col == position_of(row)).astype(jnp.bfloat16)
        select_cols = (row == position_of(col)).astype(jnp.bfloat16)
        for head in range(SGU_HEADS):
            w = jnp.where(row >= col, sgu_w_ref[head], 0.0).astype(jnp.bfloat16)
            w = jnp.dot(w, select_cols, preferred_element_type=jnp.float32).astype(jnp.bfloat16)
            sgu_wp_scr[head] = jnp.dot(select_rows, w, preferred_element_type=jnp.float32
                                       ).astype(jnp.bfloat16)

    modulated_norm(x_next_ref, b_next, h_next_scr)
    gate = mod_ref[2, pl.ds(b, 1), :]

    def mix(j):
        col = j * LANES
        lanes = slice(col, col + LANES)
        p = p_scr.at[j % P_SLOTS]
        xc_s, xcb_s, vn_s, pre_s, z_s = (
            r.at[j % 2] for r in (xc_scr, xcb_scr, vn_scr, pre_scr, z_scr))

        def sl(k, r0):
            return p[r0:r0 + rows, k * LANES:(k + 1) * LANES]

        def sl3(k, r0):
            return sl(k, r0).reshape(SUBLANES, SUBLANES, LANES)

        rx_tail = rx_carry[:, :, lanes]
        for c in range(n_row_chunks):
            r0 = c * rows
            xcp, rx_tail = _causal_conv_permuted(sl3(R_X, r0), rx_tail, lru_cw_ref, col)
            xc = (xcp + lru_cb_ref[:, lanes]).reshape(rows, LANES)
            xc_s[r0:r0 + rows, :] = xc
            xcb_s[r0:r0 + rows, :] = xc.astype(jnp.bfloat16)
            v = sl(S_V, r0)
            vc = v - jnp.mean(v, axis=-1, keepdims=True)
            var = jnp.mean(vc * vc, axis=-1, keepdims=True)
            vn_s[r0:r0 + rows, :] = (vc * lax.rsqrt(var + EPS)).astype(jnp.bfloat16)
        rx_carry[:, :, lanes] = rx_tail

        pre_s[...] = jnp.dot(xcb_s[...], lru_w_ref[j], preferred_element_type=jnp.float32)
        vn_wide = jnp.concatenate(
            [vn_s[n * SGU_CHUNK:(n + 1) * SGU_CHUNK, :] for n in range(n_sgu_chunks)], axis=1)
        z_s[...] = jnp.dot(sgu_wp_scr[j], vn_wide, preferred_element_type=jnp.float32)
        yield

        lam = lam_ref[:, lanes]
        softplus_neg_lam = jnp.maximum(-lam, 0.0) + jnp.log1p(jnp.exp(-jnp.abs(lam)))
        log2_a_per_r = (-LRU_C * LOG2_E) * softplus_neg_lam
        cx_tail = cx_carry[:, :, lanes]
        state = h_carry[:, lanes]
        for c in range(n_row_chunks):
            r0 = c * rows
            chunk, q0 = divmod(r0, SGU_CHUNK)

            r = _sigmoid(pre_s[r0:r0 + rows, :LANES] + lru_ba_ref[:, lanes])
            i = _sigmoid(pre_s[r0:r0 + rows, LANES:] + lru_bx_ref[:, lanes])
            a = jnp.exp2(r * log2_a_per_r)
            y = 1.0 - a * a
            bb = (y * lax.rsqrt(jnp.maximum(y, 1e-30))) * (i * xc_s[r0:r0 + rows, :])
            hsp, state = _linear_scan_permuted(a.reshape(SUBLANES, SUBLANES, LANES),
                                               bb.reshape(SUBLANES, SUBLANES, LANES), state)
            merged = _sigmoid(sl(G_R, r0)) * (_silu(sl(R_Z, r0)) * hsp.reshape(rows, LANES))

            convp, cx_tail = _causal_conv_permuted(sl3(A_C, r0) * sl3(A_X, r0), cx_tail,
                                                   conv_a_ref, col)
            conv = convp.reshape(rows, LANES)
            merged = merged + _sigmoid(sl(G_A, r0)) * ((_silu(sl(A_Z, r0)) * sl(A_B, r0)) * conv)

            z = (z_s[q0:q0 + rows, chunk * LANES:(chunk + 1) * LANES]
                 + sgu_b_ref[j, q0:q0 + rows, :])
            merged = merged + _sigmoid(sl(G_S, r0)) * ((_silu(sl(S_Z, r0)) * sl(S_U, r0)) * z)

            m_scr[r0:r0 + rows, lanes] = merged.astype(jnp.bfloat16)
        cx_carry[:, :, lanes] = cx_tail
        h_carry[:, lanes] = state

    for j in range(N_COL_BLOCKS):
        upcoming = (j + 1, h_scr) if j + 1 < N_COL_BLOCKS else (0, h_next_scr)
        mix_parts = mix(j)
        project(*upcoming, 0)
        next(mix_parts)
        project(*upcoming, 1)
        next(mix_parts, None)

    res = p_scr.at[(N_COL_BLOCKS - 1) % P_SLOTS]
    res[:, :D_MODEL] = jnp.dot(m_scr[...], w_out_ref[...], preferred_element_type=jnp.float32)
    for c in range(n_row_chunks):
        r0 = c * rows
        out = residual_chunk(x_ref, c) + gate * res[r0:r0 + rows, :D_MODEL]
        if final_norm:
            inv_o = lax.rsqrt(jnp.mean(out * out, axis=-1, keepdims=True) + EPS)
            out = (out * inv_o) * fgain_ref[...]
        if unpermute_out:
            out = _permute_chunk(perm_scr, c, out, dyn_zero)
        o_ref[0, r0:r0 + rows, :] = out
    h_scr[...] = h_next_scr[...]


def _layer_resident(arr, layer):
    index = (layer,) + (0,) * (arr.ndim - 1)
    return pl.BlockSpec((1,) + arr.shape[1:], lambda i: index, pipeline_mode=pl.Buffered(1))


def _tile_index(i, *, seq_tiles, last, ahead):
    t = jnp.minimum(i + ahead, last)
    return (t // seq_tiles, t % seq_tiles, 0)


def _layer(x, layer, stacked, w_in_blocked, w_in_raw, final_gain, *, last):
    bsz, seq, d = x.shape
    tile = SEQ_TILE
    seq_tiles = seq // tile
    n_steps = bsz * seq_tiles
    prep_next = layer + 1 < w_in_raw.shape[0]
    slab = d // n_steps
    tile_spec = functools.partial(_tile_index, seq_tiles=seq_tiles, last=n_steps - 1)
    operands =[(arr, layer) for arr in stacked]
    operands.insert(2, (w_in_blocked[None], 0))
    in_specs = [pl.BlockSpec((1, tile, d), functools.partial(tile_spec, ahead=0)),
                pl.BlockSpec((1, tile, d), functools.partial(tile_spec, ahead=1))]
    inputs = [x, x]
    out_specs = [pl.BlockSpec((1, tile, d), functools.partial(tile_spec, ahead=0))]
    out_shape = [jax.ShapeDtypeStruct(x.shape, x.dtype)]
    if prep_next:
        in_specs.append(pl.BlockSpec((1, slab, w_in_raw.shape[2]), lambda i: (layer + 1, i, 0)))
        inputs.append(w_in_raw)
        out_specs.append(pl.BlockSpec((N_COL_BLOCKS, slab, BLOCK_COLS), lambda i: (0, i, 0)))
        out_shape.append(jax.ShapeDtypeStruct((N_COL_BLOCKS, d, BLOCK_COLS), jnp.bfloat16))
    in_specs += [_layer_resident(arr, index) for arr, index in operands]
    in_specs.append(pl.BlockSpec(final_gain.shape, lambda i: (0, 0), pipeline_mode=pl.Buffered(1)))
    inputs += [arr for arr, _ in operands] + [final_gain]
    outs = pl.pallas_call(
        functools.partial(_layer_kernel, final_norm=last, seq_tiles=seq_tiles,
                          prep_next=prep_next, permute_in=(layer == 0), unpermute_out=last),
        grid=(n_steps,),
        in_specs=in_specs,
        out_specs=out_specs,
        out_shape=out_shape,
        scratch_shapes=[
            pltpu.VMEM((tile, d), jnp.bfloat16),
            pltpu.VMEM((tile, d), jnp.bfloat16),
            pltpu.VMEM((tile, d), jnp.bfloat16),
            pltpu.VMEM((P_SLOTS, tile, BLOCK_COLS), jnp.float32),
            pltpu.VMEM((2, tile, LANES), jnp.float32),
            pltpu.VMEM((2, tile, LANES), jnp.bfloat16),
            pltpu.VMEM((2, tile, LANES), jnp.bfloat16),
            pltpu.VMEM((2, tile, 2 * LANES), jnp.float32),
            pltpu.VMEM((2, SGU_CHUNK, tile), jnp.float32),
            pltpu.VMEM((SGU_HEADS, SGU_CHUNK, SGU_CHUNK), jnp.bfloat16),
            pltpu.VMEM((2, tile // ROW_CHUNK * PERM_ROWS, LANES), jnp.float32),
            pltpu.VMEM((CONV_K - 1, SUBLANES, d), jnp.float32),
            pltpu.VMEM((LRU_CONV_K - 1, SUBLANES, d), jnp.float32),
            pltpu.VMEM((1, d), jnp.float32),
        ],
        compiler_params=pltpu.CompilerParams(
            dimension_semantics=("arbitrary",),
            vmem_limit_bytes=VMEM_LIMIT_BYTES),
        name="hybrid_mixer_layer",
    )(*inputs)
    return (outs[0], outs[1]) if prep_next else (outs[0], None)


def _block_columns_kernel(*refs):
    o_ref = refs[-1]
    for k in range(N_SLICES):
        o_ref[0, :, k * LANES:(k + 1) * LANES] = refs[k][0].astype(jnp.bfloat16)


def _slice_block_index(j, *, layer, k):
    return (layer, 0, k * N_COL_BLOCKS + j)


def _block_columns(w_in, layer):
    _, d, _ = w_in.shape
    return pl.pallas_call(
        _block_columns_kernel,
        grid=(N_COL_BLOCKS,),
        in_specs=[pl.BlockSpec((1, d, LANES), functools.partial(_slice_block_index, layer=layer, k=k))
                  for k in range(N_SLICES)],
        out_specs=pl.BlockSpec((1, d, BLOCK_COLS), lambda j: (j, 0, 0)),
        out_shape=jax.ShapeDtypeStruct((N_COL_BLOCKS, d, BLOCK_COLS), jnp.bfloat16),
        name="block_columns_bf16",
    )(*([w_in] * N_SLICES))


def _permute_positions(arr, axis):
    axis = axis % arr.ndim
    shape = arr.shape
    split = arr.reshape(shape[:axis] + (SGU_CHUNK // ROW_CHUNK, SUBLANES, SUBLANES) + shape[axis + 1:])
    return jnp.swapaxes(split, axis + 1, axis + 2).reshape(shape)


def _pair_block_diag(w_a, w_x):
    def pairs(w):
        w = w.reshape(w.shape[0], LRU_HEADS // 2, 2, LRU_HEAD_DIM, LRU_HEAD_DIM)
        zero = jnp.zeros_like(w[:, :, 0])
        top = jnp.concatenate([w[:, :, 0], zero], axis=-1)
        bottom = jnp.concatenate([zero, w[:, :, 1]], axis=-1)
        return jnp.concatenate([top, bottom], axis=-2)
    return jnp.concatenate([pairs(w_a), pairs(w_x)], axis=-1)


def kernel(x, c, norm_gain, w_mod, b_mod, w_in, w_out, conv_a_w, sgu_w, sgu_b, lru_conv_w,
           lru_conv_b, lru_wa, lru_ba, lru_wx, lru_bx, lru_lambda, final_gain):
    depth = w_in.shape[0]
    d = x.shape[-1]
    stacked = (
        _modulation(c, w_mod, b_mod),
        norm_gain.reshape(depth, 1, d),
        w_out.astype(jnp.bfloat16),
        conv_a_w,
        sgu_w,
        jnp.broadcast_to(_permute_positions(sgu_b, -1)[..., None], sgu_b.shape + (LANES,)),
        lru_conv_w,
        lru_conv_b.reshape(depth, 1, d),
        _pair_block_diag(lru_wa, lru_wx).astype(jnp.bfloat16),
        lru_ba.reshape(depth, 1, d),
        lru_bx.reshape(depth, 1, d),
        lru_lambda.reshape(depth, 1, d),
    )
    w_in_blocked = _block_columns(w_in, 0)
    for l in range(depth):
        x, w_in_blocked = _layer(x, l, stacked, w_in_blocked, w_in, final_gain.reshape(1, d),
                                 last=(l == depth - 1))
    return x
```
